```python
import jax, jax.numpy as jnp
from jax import lax
import numpy as np

D_MODEL = 2048
BATCH = 1
SEQ = 16384
DEPTH = 4
DEC_BATCH = 2
DEC_SEQ = 8192
PAST_LEN = 128

N_MIXERS = 2
N_CONV = (DEPTH + 1) // 2
N_ATTN = DEPTH // 2
N_HEADS = 16
HEAD_DIM = D_MODEL // N_HEADS
N_KV_HEADS = 4
GROUP = N_HEADS // N_KV_HEADS
ROT_DIM = HEAD_DIM // 4
ROPE_THETA = 500000.0
WINDOW = 128
BLOCK = 128
CONV_WIDTH = 31
D_FF = -(-8 * D_MODEL // (3 * 256)) * 256
EPS = 1e-6
NEG = -1e30

kernel_name = "hybrid_conformer_swa_encoder"


def _rmsnorm(x, g):
    xf = x.astype(jnp.float32)
    y = xf * lax.rsqrt(jnp.mean(xf * xf, axis=-1, keepdims=True) + EPS)
    return (y * g.astype(jnp.float32)).astype(x.dtype)


def _layernorm(x, g, b):
    xf = x.astype(jnp.float32)
    mu = jnp.mean(xf, axis=-1, keepdims=True)
    xc = xf - mu
    y = xc * lax.rsqrt(jnp.mean(xc * xc, axis=-1, keepdims=True) + EPS)
    return (y * g.astype(jnp.float32) + b.astype(jnp.float32)).astype(x.dtype)


def _partial_rope(x, pos):
    inv_freq = ROPE_THETA ** (-jnp.arange(0, ROT_DIM, 2, dtype=jnp.float32) / ROT_DIM)
    ang = pos[:, None] * inv_freq[None, :]
    cos = jnp.concatenate([jnp.cos(ang), jnp.cos(ang)], -1)[None, :, None, :]
    sin = jnp.concatenate([jnp.sin(ang), jnp.sin(ang)], -1)[None, :, None, :]
    xr = x[..., :ROT_DIM].astype(jnp.float32)
    x1, x2 = jnp.split(xr, 2, axis=-1)
    rot = jnp.concatenate([-x2, x1], axis=-1)
    xr = (xr * cos + rot * sin).astype(x.dtype)
    return jnp.concatenate([xr, x[..., ROT_DIM:]], axis=-1)


def _conv_module(h, w_pw1, w_dw, b_dw, ln_g, ln_b, w_pw2):
    a, g = jnp.split(h @ w_pw1, 2, axis=-1)
    u = a * jax.nn.sigmoid(g)
    u = lax.conv_general_dilated(
        u, w_dw[:, None, :], window_strides=(1,),
        padding=[(CONV_WIDTH // 2, CONV_WIDTH // 2)],
        dimension_numbers=('NWC', 'WIO', 'NWC'),
        feature_group_count=D_MODEL) + b_dw
    u = jax.nn.silu(_layernorm(u, ln_g, ln_b))
    return u @ w_pw2


def _band(t, nb):
    b = t.shape[0]
    tp = jnp.pad(t, ((0, 0), (BLOCK, BLOCK), (0, 0), (0, 0)))
    tp = tp.reshape(b, nb + 2, BLOCK, N_KV_HEADS, HEAD_DIM)
    return jnp.concatenate([tp[:, :-2], tp[:, 1:-1], tp[:, 2:]], axis=2)


def _window_attention(h, w_q, w_k, w_v, w_o, sink):
    b, s, _ = h.shape
    nb = s // BLOCK
    pos = jnp.arange(s, dtype=jnp.float32)
    q = _partial_rope((h @ w_q).reshape(b, s, N_HEADS, HEAD_DIM), pos)
    k = _partial_rope((h @ w_k).reshape(b, s, N_KV_HEADS, HEAD_DIM), pos)
    v = (h @ w_v).reshape(b, s, N_KV_HEADS, HEAD_DIM)
    qb = q.reshape(b, nb, BLOCK, N_KV_HEADS, GROUP, HEAD_DIM)
    kb, vb = _band(k, nb), _band(v, nb)
    sc = jnp.einsum('bnqkgd,bnjkd->bnkgqj', qb, kb,
                    preferred_element_type=jnp.float32) * (HEAD_DIM ** -0.5)
    blk = jnp.arange(nb)[:, None, None]
    q_pos = blk * BLOCK + jnp.arange(BLOCK)[None, :, None]
    k_pos = blk * BLOCK - BLOCK + jnp.arange(3 * BLOCK)[None, None, :]
    valid = (jnp.abs(q_pos - k_pos) <= WINDOW) & (k_pos >= 0) & (k_pos < s)
    sc = jnp.where(valid[None, :, None, None, :, :], sc, NEG)
    sink_l = sink.astype(jnp.float32).reshape(1, 1, N_KV_HEADS, GROUP, 1, 1)
    m = jnp.maximum(jnp.max(sc, axis=-1, keepdims=True), sink_l)
    p = jnp.exp(sc - m)
    p = p / (jnp.sum(p, axis=-1, keepdims=True) + jnp.exp(sink_l - m))
    o = jnp.einsum('bnkgqj,bnjkd->bnqkgd', p.astype(vb.dtype), vb)
    return o.reshape(b, s, D_MODEL) @ w_o


def _swiglu(h, w_gate, w_up, w_down):
    return (jax.nn.silu(h @ w_gate) * (h @ w_up)) @ w_down


def _trunk(x, c, w_ada, b_ada, norm_g, w_gate, w_up, w_down,
           conv_w_pw1, conv_w_dw, conv_b_dw, conv_ln_g, conv_ln_b, conv_w_pw2,
           attn_w_q, attn_w_k, attn_w_v, attn_w_o, attn_sink):
    c_act = jax.nn.silu(c)
    for i in range(DEPTH):
        mod = c_act @ w_ada[i] + b_ada[i]
        sh_m, sc_m, gt_m, sh_f, sc_f, gt_f = [t[:, None, :] for t in jnp.split(mod, 6, axis=-1)]
        h = _rmsnorm(x, norm_g[i, 0]) * (1.0 + sc_m) + sh_m
        j = i // N_MIXERS
        if i % N_MIXERS == 0:
            out = _conv_module(h, conv_w_pw1[j], conv_w_dw[j], conv_b_dw[j],
                               conv_ln_g[j], conv_ln_b[j], conv_w_pw2[j])
        else:
            out = _window_attention(h, attn_w_q[j], attn_w_k[j], attn_w_v[j],
                                    attn_w_o[j], attn_sink[j])
        x = x + gt_m * _rmsnorm(out, norm_g[i, 1])
        h = _rmsnorm(x, norm_g[i, 2]) * (1.0 + sc_f) + sh_f
        x = x + gt_f * _rmsnorm(_swiglu(h, w_gate[i], w_up[i], w_down[i]), norm_g[i, 3])
    return x


def setup_inputs(seed: int = 0) -> dict:
    key = jax.random.key(seed)
    ks = jax.random.split(key, 24)
    f32 = jnp.float32
    nrm = lambda k, shape, scale: jax.random.normal(k, shape, f32) * scale
    D = D_MODEL
    return {
        "x_prompt": nrm(ks[0], (BATCH, SEQ, D), 1.0),
        "x_sample": nrm(ks[1], (DEC_BATCH, DEC_SEQ, D), 1.0),
        "c_prompt": nrm(ks[2], (BATCH, D), 1.0),
        "c_sample": nrm(ks[3], (DEC_BATCH, D), 1.0),
        "w_ada": nrm(ks[4], (DEPTH, D, 6 * D), D ** -0.5),
        "b_ada": nrm(ks[5], (DEPTH, 6 * D), 0.01),
        "norm_g": 1.0 + nrm(ks[6], (DEPTH, 4, D), 0.05),
        "w_gate": nrm(ks[7], (DEPTH, D, D_FF), D ** -0.5),
        "w_up": nrm(ks[8], (DEPTH, D, D_FF), D ** -0.5),
        "w_down": nrm(ks[9], (DEPTH, D_FF, D), D_FF ** -0.5),
        "conv_w_pw1": nrm(ks[10], (N_CONV, D, 2 * D), D ** -0.5),
        "conv_w_dw": nrm(ks[11], (N_CONV, CONV_WIDTH, D), CONV_WIDTH ** -0.5),
        "conv_b_dw": nrm(ks[12], (N_CONV, D), 0.01),
        "conv_ln_g": 1.0 + nrm(ks[13], (N_CONV, D), 0.05),
        "conv_ln_b": nrm(ks[14], (N_CONV, D), 0.01),
        "conv_w_pw2": nrm(ks[15], (N_CONV, D, D), D ** -0.5),
        "attn_w_q": nrm(ks[16], (N_ATTN, D, N_HEADS * HEAD_DIM), D ** -0.5),
        "attn_w_k": nrm(ks[17], (N_ATTN, D, N_KV_HEADS * HEAD_DIM), D ** -0.5),
        "attn_w_v": nrm(ks[18], (N_ATTN, D, N_KV_HEADS * HEAD_DIM), D ** -0.5),
        "attn_w_o": nrm(ks[19], (N_ATTN, N_HEADS * HEAD_DIM, D), (N_HEADS * HEAD_DIM) ** -0.5),
        "attn_sink": nrm(ks[20], (N_ATTN, N_HEADS), 0.5),
    }


def reference(x_prompt, x_sample, c_prompt, c_sample, w_ada, b_ada, norm_g,
              w_gate, w_up, w_down, conv_w_pw1, conv_w_dw, conv_b_dw, conv_ln_g,
              conv_ln_b, conv_w_pw2, attn_w_q, attn_w_k, attn_w_v, attn_w_o, attn_sink):
    y_prompt = _trunk(x_prompt, c_prompt, w_ada, b_ada, norm_g, w_gate, w_up, w_down,
                      conv_w_pw1, conv_w_dw, conv_b_dw, conv_ln_g, conv_ln_b, conv_w_pw2,
                      attn_w_q, attn_w_k, attn_w_v, attn_w_o, attn_sink)
    y_sample = _trunk(x_sample, c_sample, w_ada, b_ada, norm_g, w_gate, w_up, w_down,
                      conv_w_pw1, conv_w_dw, conv_b_dw, conv_ln_g, conv_ln_b, conv_w_pw2,
                      attn_w_q, attn_w_k, attn_w_v, attn_w_o, attn_sink)
    return (y_prompt, y_sample)
```

```python
import functools

import jax
import jax.numpy as jnp
from jax import lax
from jax.experimental import pallas as pl
from jax.experimental.pallas import tpu as pltpu

N_HEADS = 16
N_KV_HEADS = 4
GROUP = N_HEADS // N_KV_HEADS
HEAD_DIM = 128
ROT_DIM = HEAD_DIM // 4
ROPE_THETA = 500000.0
WINDOW = 128
CONV_WIDTH = 31
CONV_HALO = 16
EPS = 1e-6
NEG = -1e30

TOKEN_TILE = 512
FF_TILE = 512
ADA_TILE = 1536
VMEM_LIMIT_BYTES = 56 * 1024 * 1024


def _params(n_axes):
    return pltpu.CompilerParams(
        dimension_semantics=("arbitrary",) * n_axes,
        vmem_limit_bytes=VMEM_LIMIT_BYTES)


def _resident(shape):
    return pl.BlockSpec(shape, lambda *_: (0,) * len(shape),
                        pipeline_mode=pl.Buffered(1))


def _sigmoid(x):
    return 1.0 / (1.0 + jnp.exp(-x))


def _rms(xf, g):
    return xf * lax.rsqrt(jnp.mean(xf * xf, axis=-1, keepdims=True) + EPS) * g


def _modulated_norm(x, g, mod_ref, shift_row, scale_row):
    shift = mod_ref[0, shift_row:shift_row + 1, :]
    scale = mod_ref[0, scale_row:scale_row + 1, :]
    return (_rms(x, g) * (1.0 + scale) + shift).astype(jnp.bfloat16)


def _gated_residual(x, out, g, mod_ref, gate_row):
    gate = mod_ref[0, gate_row:gate_row + 1, :]
    return x + gate * _rms(out, g)


def _ada_kernel(c_ref, w_ref, b_ref, o_ref):
    c = c_ref[...]
    ca = (c * _sigmoid(c)).astype(jnp.bfloat16)
    w = w_ref[0].astype(jnp.bfloat16)
    o_ref[0] = jnp.dot(ca, w, preferred_element_type=jnp.float32) + b_ref[0]


def _ada(c_all, w_ada, b_ada):
    depth, d, n = w_ada.shape
    rows = c_all.shape[0]
    return pl.pallas_call(
        _ada_kernel,
        grid=(depth, n // ADA_TILE),
        in_specs=[
            pl.BlockSpec((rows, d), lambda i, j: (0, 0)),
            pl.BlockSpec((1, d, ADA_TILE), lambda i, j: (i, 0, j)),
            pl.BlockSpec((1, 1, ADA_TILE), lambda i, j: (i, 0, j)),
        ],
        out_specs=pl.BlockSpec((1, rows, ADA_TILE), lambda i, j: (i, 0, j)),
        out_shape=jax.ShapeDtypeStruct((depth, rows, n), jnp.float32),
        compiler_params=_params(2),
        name="ada_mod",
    )(c_all, w_ada, b_ada.reshape(depth, 1, n))


def _conv1_kernel(x_ref, mod_ref, g_ref, w_ref, u_ref):
    d = x_ref.shape[1]
    h = _modulated_norm(x_ref[...], g_ref[...], mod_ref, 0, 1)
    chunk = 512
    for c in range(0, d, chunk):
        a = jnp.dot(h, w_ref[:, c:c + chunk], preferred_element_type=jnp.float32)
        g = jnp.dot(h, w_ref[:, d + c:d + c + chunk], preferred_element_type=jnp.float32)
        u_ref[:, c:c + chunk] = a * _sigmoid(g)


def _conv1(x, mod, g0, w_pw1, seq):
    t, d = x.shape
    tm = TOKEN_TILE
    return pl.pallas_call(
        _conv1_kernel,
        grid=(t // tm,),
        in_specs=[
            pl.BlockSpec((tm, d), lambda j: (j, 0)),
            pl.BlockSpec((1, 6, d), lambda j: (j * tm // seq, 0, 0)),
            _resident((1, d)),
            _resident((d, 2 * d)),
        ],
        out_specs=pl.BlockSpec((tm, d), lambda j: (j, 0)),
        out_shape=jax.ShapeDtypeStruct((t, d), jnp.float32),
        compiler_params=_params(1),
        name="conv_pw1_glu",
    )(x, mod, g0, w_pw1)


def _conv2_kernel(seq, u_ref, up_ref, un_ref, x_ref, mod_ref, wdw_ref, bdw_ref,
                  lng_ref, lnb_ref, w_ref, g_ref, o_ref, ext_ref, y_ref):
    tm, d = x_ref.shape
    j = pl.program_id(0)
    is_start = (j * tm) % seq == 0
    is_end = ((j + 1) * tm) % seq == 0
    ext_ref[0:CONV_HALO, :] = jnp.where(is_start, 0.0, up_ref[...])
    ext_ref[CONV_HALO:CONV_HALO + tm, :] = u_ref[...]
    ext_ref[CONV_HALO + tm:, :] = jnp.where(is_end, 0.0, un_ref[...])

    rows, cols = 64, 256
    first = CONV_HALO - CONV_WIDTH // 2

    def row_block(r, carry):
        r0 = pl.multiple_of(r * rows, rows)
        for c in range(0, d, cols):
            acc = jnp.zeros((rows, cols), jnp.float32) + bdw_ref[:, c:c + cols]
            win = ext_ref[pl.ds(r0, rows + 2 * CONV_HALO), c:c + cols]
            for w in range(CONV_WIDTH):
                tap = win[first + w:first + w + rows]
                acc = acc + tap * wdw_ref[w:w + 1, c:c + cols]
            y_ref[pl.ds(r0, rows), c:c + cols] = acc
        return carry

    lax.fori_loop(0, tm // rows, row_block, 0)

    y = y_ref[...]
    mu = jnp.mean(y, axis=-1, keepdims=True)
    yc = y - mu
    z = yc * lax.rsqrt(jnp.mean(yc * yc, axis=-1, keepdims=True) + EPS)
    z = z * lng_ref[...] + lnb_ref[...]
    s = (z * _sigmoid(z)).astype(jnp.bfloat16)
    out = jnp.dot(s, w_ref[...], preferred_element_type=jnp.float32)
    o_ref[...] = _gated_residual(x_ref[...], out, g_ref[...], mod_ref, 2)


def _conv2(u, x, mod, w_dw, b_dw, ln_g, ln_b, w_pw2, g1, seq):
    t, d = x.shape
    tm = TOKEN_TILE
    per = tm // CONV_HALO
    last = t // CONV_HALO - 1
    return pl.pallas_call(
        functools.partial(_conv2_kernel, seq),
        grid=(t // tm,),
        in_specs=[
            pl.BlockSpec((tm, d), lambda j: (j, 0)),
            pl.BlockSpec((CONV_HALO, d), lambda j: (jnp.maximum(j * per - 1, 0), 0)),
            pl.BlockSpec((CONV_HALO, d), lambda j: (jnp.minimum((j + 1) * per, last), 0)),
            pl.BlockSpec((tm, d), lambda j: (j, 0)),
            pl.BlockSpec((1, 6, d), lambda j: (j * tm // seq, 0, 0)),
            _resident((CONV_WIDTH, d)),
            _resident((1, d)),
            _resident((1, d)),
            _resident((1, d)),
            _resident((d, d)),
            _resident((1, d)),
        ],
        out_specs=pl.BlockSpec((tm, d), lambda j: (j, 0)),
        out_shape=jax.ShapeDtypeStruct((t, d), jnp.float32),
        scratch_shapes=[
            pltpu.VMEM((tm + 2 * CONV_HALO, d), jnp.float32),
            pltpu.VMEM((tm, d), jnp.float32),
        ],
        compiler_params=_params(1),
        name="conv_dw_pw2",
    )(u, u, u, x, mod, w_dw, b_dw, ln_g, ln_b, w_pw2, g1)


def _rope(x, cos, sin_signed):
    half = ROT_DIM // 2
    lane = lax.broadcasted_iota(jnp.int32, x.shape, 1)
    rot = jnp.where(lane < half,
                    pltpu.roll(x, HEAD_DIM - half, axis=1),
                    pltpu.roll(x, half, axis=1))
    return x * cos + rot * sin_signed


def _qkv_kernel(x_ref, mod_ref, g_ref, w_ref, cos_ref, sin_ref, q_ref, k_ref, v_ref):
    nq = q_ref.shape[1]
    nk = k_ref.shape[1]
    h = _modulated_norm(x_ref[...], g_ref[...], mod_ref, 0, 1)
    cos = cos_ref[...]
    sin = sin_ref[...]
    chunk = 4 * HEAD_DIM
    for c in range(0, nq + nk, chunk):
        y = jnp.dot(h, w_ref[:, c:c + chunk], preferred_element_type=jnp.float32)
        for hd in range(0, chunk, HEAD_DIM):
            r = _rope(y[:, hd:hd + HEAD_DIM], cos, sin).astype(jnp.bfloat16)
            if c < nq:
                q_ref[:, c + hd:c + hd + HEAD_DIM] = r
            else:
                k_ref[:, c - nq + hd:c - nq + hd + HEAD_DIM] = r
    v = jnp.dot(h, w_ref[:, nq + nk:], preferred_element_type=jnp.float32)
    v_ref[...] = v.astype(jnp.bfloat16)


def _qkv(x, mod, g0, w_qkv, cos, sin, seq):
    t, d = x.shape
    tm = TOKEN_TILE
    nq = N_HEADS * HEAD_DIM
    nk = N_KV_HEADS * HEAD_DIM
    tiles_per_seq = seq // tm
    return pl.pallas_call(
        _qkv_kernel,
        grid=(t // tm,),
        in_specs=[
            pl.BlockSpec((tm, d), lambda j: (j, 0)),
            pl.BlockSpec((1, 6, d), lambda j: (j * tm // seq, 0, 0)),
            _resident((1, d)),
            _resident((d, nq + 2 * nk)),
            pl.BlockSpec((tm, HEAD_DIM), lambda j: (j % tiles_per_seq, 0)),
            pl.BlockSpec((tm, HEAD_DIM), lambda j: (j % tiles_per_seq, 0)),
        ],
        out_specs=[
            pl.BlockSpec((tm, nq), lambda j: (j, 0)),
            pl.BlockSpec((tm, nk), lambda j: (j, 0)),
            pl.BlockSpec((tm, nk), lambda j: (j, 0)),
        ],
        out_shape=[
            jax.ShapeDtypeStruct((t, nq), jnp.bfloat16),
            jax.ShapeDtypeStruct((t, nk), jnp.bfloat16),
            jax.ShapeDtypeStruct((t, nk), jnp.bfloat16),
        ],
        compiler_params=_params(1),
        name="attn_qkv_rope",
    )(x, mod, g0, w_qkv, cos, sin)


def _attn_kernel(seq, sink_ref, q_ref, k_ref, kp_ref, kn_ref, v_ref, vp_ref, vn_ref,
                 x_ref, mod_ref, w_ref, g_ref, o_ref, kext_ref, vext_ref, oh_ref):
    tm = x_ref.shape[0]
    blk = WINDOW
    nblk = tm // blk
    j = pl.program_id(0)
    is_start = (j * tm) % seq == 0
    is_end = ((j + 1) * tm) % seq == 0

    kext_ref[0:blk, :] = kp_ref[...]
    kext_ref[blk:blk + tm, :] = k_ref[...]
    kext_ref[blk + tm:, :] = kn_ref[...]
    vext_ref[0:blk, :] = vp_ref[...]
    vext_ref[blk:blk + tm, :] = v_ref[...]
    vext_ref[blk + tm:, :] = vn_ref[...]

    rows = GROUP * blk
    keys = 3 * blk
    qi = lax.broadcasted_iota(jnp.int32, (rows, keys), 0) & (blk - 1)
    kj = lax.broadcasted_iota(jnp.int32, (rows, keys), 1)
    rel = kj - blk - qi
    band = (rel >= -WINDOW) & (rel <= WINDOW)
    scale = HEAD_DIM ** -0.5

    def q_block(b, carry):
        r0 = pl.multiple_of(b * blk, blk)
        lo = jnp.where(jnp.logical_and(is_start, b == 0), blk, 0)
        hi = jnp.where(jnp.logical_and(is_end, b == nblk - 1), 2 * blk, keys)
        valid = band & (kj >= lo) & (kj < hi)
        for g in range(N_KV_HEADS):
            qs = jnp.concatenate(
                [q_ref[pl.ds(r0, blk), (g * GROUP + h) * HEAD_DIM:(g * GROUP + h + 1) * HEAD_DIM]
                 for h in range(GROUP)], axis=0)
            kb = kext_ref[pl.ds(r0, keys), g * HEAD_DIM:(g + 1) * HEAD_DIM]
            vb = vext_ref[pl.ds(r0, keys), g * HEAD_DIM:(g + 1) * HEAD_DIM]
            sc = lax.dot_general(qs, kb, (((1,), (1,)), ((), ())),
                                 preferred_element_type=jnp.float32) * scale
            sc = jnp.where(valid, sc, NEG)
            sink = jnp.concatenate(
                [jnp.full((blk, 1), sink_ref[g * GROUP + h], jnp.float32) for h in range(GROUP)],
                axis=0)
            m = jnp.maximum(jnp.max(sc, axis=-1, keepdims=True), sink)
            p = jnp.exp(sc - m)
            denom = jnp.sum(p, axis=-1, keepdims=True) + jnp.exp(sink - m)
            o = jnp.dot(p.astype(jnp.bfloat16), vb, preferred_element_type=jnp.float32) / denom
            for h in range(GROUP):
                oh_ref[pl.ds(r0, blk), (g * GROUP + h) * HEAD_DIM:(g * GROUP + h + 1) * HEAD_DIM] = (
                    o[h * blk:(h + 1) * blk].astype(jnp.bfloat16))
        return carry

    lax.fori_loop(0, nblk, q_block, 0)

    out = jnp.dot(oh_ref[...], w_ref[...], preferred_element_type=jnp.float32)
    o_ref[...] = _gated_residual(x_ref[...], out, g_ref[...], mod_ref, 2)


def _attn(q, k, v, x, mod, w_o, g1, sink, seq):
    t, d = x.shape
    tm = TOKEN_TILE
    nq = q.shape[1]
    nk = k.shape[1]
    per = tm // WINDOW
    last = t // WINDOW - 1
    main = lambda j: (j, 0)
    prev = lambda j: (jnp.maximum(j * per - 1, 0), 0)
    nxt = lambda j: (jnp.minimum((j + 1) * per, last), 0)
    return pl.pallas_call(
        functools.partial(_attn_kernel, seq),
        grid=(t // tm,),
        in_specs=[
            pl.BlockSpec(memory_space=pltpu.SMEM),
            pl.BlockSpec((tm, nq), main),
            pl.BlockSpec((tm, nk), main),
            pl.BlockSpec((WINDOW, nk), prev),
            pl.BlockSpec((WINDOW, nk), nxt),
            pl.BlockSpec((tm, nk), main),
            pl.BlockSpec((WINDOW, nk), prev),
            pl.BlockSpec((WINDOW, nk), nxt),
            pl.BlockSpec((tm, d), main),
            pl.BlockSpec((1, 6, d), lambda j: (j * tm // seq, 0, 0)),
            _resident((nq, d)),
            _resident((1, d)),
        ],
        out_specs=pl.BlockSpec((tm, d), main),
        out_shape=jax.ShapeDtypeStruct((t, d), jnp.float32),
        scratch_shapes=[
            pltpu.VMEM((tm + 2 * WINDOW, nk), jnp.bfloat16),
            pltpu.VMEM((tm + 2 * WINDOW, nk), jnp.bfloat16),
            pltpu.VMEM((tm, nq), jnp.bfloat16),
        ],
        compiler_params=_params(1),
        name="attn_band_out",
    )(sink, q, k, k, k, v, v, v, x, mod, w_o, g1)


def _ffn_kernel(x_ref, mod_ref, g2_ref, g3_ref, wgu_ref, wd_ref, o_ref, h_ref, acc_ref):
    f = pl.program_id(1)
    tf = wd_ref.shape[1]

    @pl.when(f == 0)
    def _():
        h_ref[...] = _modulated_norm(x_ref[...], g2_ref[...], mod_ref, 3, 4)
        acc_ref[...] = jnp.zeros_like(acc_ref)

    gu = jnp.dot(h_ref[...], wgu_ref[0], preferred_element_type=jnp.float32)
    g = gu[:, :tf]
    u = gu[:, tf:]
    a = (g * _sigmoid(g) * u).astype(jnp.bfloat16)
    acc_ref[...] += jnp.dot(a, wd_ref[0], preferred_element_type=jnp.float32)

    @pl.when(f == pl.num_programs(1) - 1)
    def _():
        o_ref[...] = _gated_residual(x_ref[...], acc_ref[...], g3_ref[...], mod_ref, 5)


def _ffn(x, mod, g2, g3, w_gu, w_d, seq):
    t, d = x.shape
    tm = TOKEN_TILE
    nf, _, tf2 = w_gu.shape
    tf = tf2 // 2
    return pl.pallas_call(
        _ffn_kernel,
        grid=(t // tm, nf),
        in_specs=[
            pl.BlockSpec((tm, d), lambda j, f: (j, 0)),
            pl.BlockSpec((1, 6, d), lambda j, f: (j * tm // seq, 0, 0)),
            _resident((1, d)),
            _resident((1, d)),
            pl.BlockSpec((1, d, tf2), lambda j, f: (f, 0, 0)),
            pl.BlockSpec((1, tf, d), lambda j, f: (f, 0, 0)),
        ],
        out_specs=pl.BlockSpec((tm, d), lambda j, f: (j, 0)),
        out_shape=jax.ShapeDtypeStruct((t, d), jnp.float32),
        scratch_shapes=[
            pltpu.VMEM((tm, d), jnp.bfloat16),
            pltpu.VMEM((tm, d), jnp.float32),
        ],
        compiler_params=_params(2),
        name="ffn_swiglu",
    )(x, mod, g2, g3, w_gu, w_d)


def _rope_tables(seq):
    inv_freq = ROPE_THETA ** (-jnp.arange(0, ROT_DIM, 2, dtype=jnp.float32) / ROT_DIM)
    ang = jnp.arange(seq, dtype=jnp.float32)[:, None] * inv_freq[None, :]
    pad = HEAD_DIM - ROT_DIM
    cos = jnp.concatenate([jnp.cos(ang), jnp.cos(ang), jnp.ones((seq, pad), jnp.float32)], -1)
    sin = jnp.concatenate([-jnp.sin(ang), jnp.sin(ang), jnp.zeros((seq, pad), jnp.float32)], -1)
    return cos, sin


def _trunk(x, mods, seq, weights):
    (norm_g, w_gu, w_d, conv_w_pw1, conv_w_dw, conv_b_dw, conv_ln_g, conv_ln_b,
     conv_w_pw2, attn_w_qkv, attn_w_o, attn_sink) = weights
    depth = norm_g.shape[0]
    d = x.shape[1]
    cos, sin = _rope_tables(seq)
    for i in range(depth):
        mod = mods[i]
        g = [norm_g[i, n].reshape(1, d) for n in range(4)]
        j = i // 2
        if i % 2 == 0:
            u = _conv1(x, mod, g[0], conv_w_pw1[j], seq)
            x = _conv2(u, x, mod, conv_w_dw[j], conv_b_dw[j].reshape(1, d),
                       conv_ln_g[j].reshape(1, d), conv_ln_b[j].reshape(1, d),
                       conv_w_pw2[j], g[1], seq)
        else:
            q, k, v = _qkv(x, mod, g[0], attn_w_qkv[j], cos, sin, seq)
            x = _attn(q, k, v, x, mod, attn_w_o[j], g[1], attn_sink[j], seq)
        x = _ffn(x, mod, g[2], g[3], w_gu[i], w_d[i], seq)
    return x


def kernel(x_prompt, x_sample, c_prompt, c_sample, w_ada, b_ada, norm_g, w_gate, w_up, w_down,
           conv_w_pw1, conv_w_dw, conv_b_dw, conv_ln_g, conv_ln_b, conv_w_pw2,
           attn_w_q, attn_w_k, attn_w_v, attn_w_o, attn_sink):
    bf16 = jnp.bfloat16
    depth, d, d_ff = w_gate.shape
    bp, sp, _ = x_prompt.shape
    bs, ss, _ = x_sample.shape
    assert sp % TOKEN_TILE == 0 and ss % TOKEN_TILE == 0 and d_ff % FF_TILE == 0
    assert TOKEN_TILE % WINDOW == 0 and (6 * d) % ADA_TILE == 0

    n_rows = bp + bs
    rows = -(-n_rows // 8) * 8
    c_all = jnp.concatenate(
        [c_prompt, c_sample, jnp.zeros((rows - n_rows, d), jnp.float32)], axis=0)
    mod_all = _ada(c_all, w_ada, b_ada).reshape(depth, rows, 6, d)
    mods_p = [mod_all[i, :bp] for i in range(depth)]
    mods_s = [mod_all[i, bp:n_rows] for i in range(depth)]

    nf = d_ff // FF_TILE
    wg = w_gate.astype(bf16).reshape(depth, d, nf, FF_TILE)
    wu = w_up.astype(bf16).reshape(depth, d, nf, FF_TILE)
    w_gu = jnp.concatenate([wg, wu], axis=-1).transpose(0, 2, 1, 3)
    w_d = w_down.astype(bf16).reshape(depth, nf, FF_TILE, d)
    w_qkv = jnp.concatenate([attn_w_q, attn_w_k, attn_w_v], axis=-1).astype(bf16)
    weights = (norm_g, w_gu, w_d, conv_w_pw1.astype(bf16), conv_w_dw, conv_b_dw,
               conv_ln_g, conv_ln_b, conv_w_pw2.astype(bf16), w_qkv,
               attn_w_o.astype(bf16), attn_sink)

    y_p = _trunk(x_prompt.reshape(bp * sp, d), mods_p, sp, weights)
    y_s = _trunk(x_sample.reshape(bs * ss, d), mods_s, ss, weights)
    return y_p.reshape(bp, sp, d), y_s.reshape(bs, ss, d)
```

```python
import functools

import jax
import jax.numpy as jnp
from jax import lax
from jax.experimental import pallas as pl
from jax.experimental.pallas import tpu as pltpu

N_HEADS = 16
N_KV_HEADS = 4
GROUP = N_HEADS // N_KV_HEADS
HEAD_DIM = 128
ROT_DIM = HEAD_DIM // 4
ROPE_THETA = 500000.0
WINDOW = 128
CONV_WIDTH = 31
CONV_HALO = 16
SUBLANES = 8
CONV_ROWS = 64
CONV_COLS = 128
EPS = 1e-6
NEG = -1e30

TOKEN_TILE = 512
FFN_TOKEN_TILE = 512
FF_TILE = 512
ADA_TILE = 1536
VMEM_LIMIT_BYTES = 56 * 1024 * 1024


def _params(n_axes):
    return pltpu.CompilerParams(
        dimension_semantics=("arbitrary",) * n_axes,
        vmem_limit_bytes=VMEM_LIMIT_BYTES)


def _resident(shape):
    return pl.BlockSpec(shape, lambda *_: (0,) * len(shape),
                        pipeline_mode=pl.Buffered(1))


def _resident_layer(shape, layer):
    return pl.BlockSpec((1,) + shape, lambda *_: (layer,) + (0,) * len(shape),
                        pipeline_mode=pl.Buffered(1))


def _sigmoid(x):
    return 1.0 / (1.0 + jnp.exp(-x))


def _rms(xf, g):
    return xf * lax.rsqrt(jnp.mean(xf * xf, axis=-1, keepdims=True) + EPS) * g


def _modulated_norm(x, g, mod_ref, shift_row, scale_row):
    shift = mod_ref[0, shift_row:shift_row + 1, :]
    scale = mod_ref[0, scale_row:scale_row + 1, :]
    return (_rms(x, g) * (1.0 + scale) + shift).astype(jnp.bfloat16)


def _gated_residual(x, out, g, mod_ref, gate_row):
    gate = mod_ref[0, gate_row:gate_row + 1, :]
    return x + gate * _rms(out, g)


def _ada_kernel(c_ref, w_ref, b_ref, o_ref):
    c = c_ref[...]
    ca = (c * _sigmoid(c)).astype(jnp.bfloat16)
    w = w_ref[0].astype(jnp.bfloat16)
    o_ref[0] = jnp.dot(ca, w, preferred_element_type=jnp.float32) + b_ref[0]


def _ada(c_all, w_ada, b_ada):
    depth, d, n = w_ada.shape
    rows = c_all.shape[0]
    return pl.pallas_call(
        _ada_kernel,
        grid=(depth, n // ADA_TILE),
        in_specs=[
            pl.BlockSpec((rows, d), lambda i, j: (0, 0)),
            pl.BlockSpec((1, d, ADA_TILE), lambda i, j: (i, 0, j)),
            pl.BlockSpec((1, 1, ADA_TILE), lambda i, j: (i, 0, j)),
        ],
        out_specs=pl.BlockSpec((1, rows, ADA_TILE), lambda i, j: (i, 0, j)),
        out_shape=jax.ShapeDtypeStruct((depth, rows, n), jnp.float32),
        compiler_params=_params(2),
        name="ada_mod",
    )(c_all, w_ada, b_ada.reshape(depth, 1, n))


def _conv1_kernel(x_ref, mod_ref, g_ref, w_ref, u_ref):
    d = x_ref.shape[1]
    h = _modulated_norm(x_ref[...], g_ref[...], mod_ref, 0, 1)
    chunk = 512
    for c in range(0, d, chunk):
        a = jnp.dot(h, w_ref[0, :, c:c + chunk], preferred_element_type=jnp.float32)
        g = jnp.dot(h, w_ref[0, :, d + c:d + c + chunk], preferred_element_type=jnp.float32)
        u_ref[:, c:c + chunk] = a * _sigmoid(g)


def _conv1(x, mod, g0, w_pw1, layer, seq):
    t, d = x.shape
    tm = TOKEN_TILE
    return pl.pallas_call(
        _conv1_kernel,
        grid=(t // tm,),
        in_specs=[
            pl.BlockSpec((tm, d), lambda j: (j, 0)),
            pl.BlockSpec((1, 6, d), lambda j: (j * tm // seq, 0, 0)),
            _resident((1, d)),
            _resident_layer((d, 2 * d), layer),
        ],
        out_specs=pl.BlockSpec((tm, d), lambda j: (j, 0)),
        out_shape=jax.ShapeDtypeStruct((t, d), jnp.float32),
        compiler_params=_params(1),
        name="conv_pw1_glu",
    )(x, mod, g0, w_pw1)


def _conv2_kernel(seq, u_ref, up_ref, un_ref, x_ref, mod_ref, wdw_ref, bdw_ref,
                  lng_ref, lnb_ref, w_ref, g_ref, o_ref, ext_ref, y_ref):
    tm, d = x_ref.shape
    j = pl.program_id(0)
    is_start = (j * tm) % seq == 0
    is_end = ((j + 1) * tm) % seq == 0
    ext_ref[0:CONV_HALO, :] = jnp.where(is_start, 0.0, up_ref[...])
    ext_ref[CONV_HALO:CONV_HALO + tm, :] = u_ref[...]
    ext_ref[CONV_HALO + tm:, :] = jnp.where(is_end, 0.0, un_ref[...])

    rows, cols = CONV_ROWS, CONV_COLS
    first = CONV_HALO - CONV_WIDTH // 2

    def row_block(r, carry):
        r0 = pl.multiple_of(r * rows, rows)
        for c in range(0, d, cols):
            n_win = (first + CONV_WIDTH - 1) // SUBLANES + 1
            wins = [ext_ref[pl.ds(r0 + SUBLANES * a, rows + SUBLANES), c:c + cols]
                    for a in range(n_win)]
            acc = None
            for b in range(SUBLANES):
                part = None
                for a in range((first + CONV_WIDTH - 1 - b) // SUBLANES + 1):
                    w = SUBLANES * a + b - first
                    if w < 0:
                        continue
                    term = wins[a] * wdw_ref[w:w + 1, c:c + cols]
                    part = term if part is None else part + term
                part = part[b:b + rows]
                acc = part if acc is None else acc + part
            y_ref[pl.ds(r0, rows), c:c + cols] = acc + bdw_ref[:, c:c + cols]
        return carry

    lax.fori_loop(0, tm // rows, row_block, 0)

    y = y_ref[...]
    mu = jnp.mean(y, axis=-1, keepdims=True)
    yc = y - mu
    z = yc * lax.rsqrt(jnp.mean(yc * yc, axis=-1, keepdims=True) + EPS)
    z = z * lng_ref[...] + lnb_ref[...]
    s = (z * _sigmoid(z)).astype(jnp.bfloat16)
    out = jnp.dot(s, w_ref[0], preferred_element_type=jnp.float32)
    o_ref[...] = _gated_residual(x_ref[...], out, g_ref[...], mod_ref, 2)


def _conv2(u, x, mod, w_dw, b_dw, ln_g, ln_b, w_pw2, layer, g1, seq):
    t, d = x.shape
    tm = TOKEN_TILE
    per = tm // CONV_HALO
    last = t // CONV_HALO - 1
    return pl.pallas_call(
        functools.partial(_conv2_kernel, seq),
        grid=(t // tm,),
        in_specs=[
            pl.BlockSpec((tm, d), lambda j: (j, 0)),
            pl.BlockSpec((CONV_HALO, d), lambda j: (jnp.maximum(j * per - 1, 0), 0)),
            pl.BlockSpec((CONV_HALO, d), lambda j: (jnp.minimum((j + 1) * per, last), 0)),
            pl.BlockSpec((tm, d), lambda j: (j, 0)),
            pl.BlockSpec((1, 6, d), lambda j: (j * tm // seq, 0, 0)),
            _resident((CONV_WIDTH, d)),
            _resident((1, d)),
            _resident((1, d)),
            _resident((1, d)),
            _resident_layer((d, d), layer),
            _resident((1, d)),
        ],
        out_specs=pl.BlockSpec((tm, d), lambda j: (j, 0)),
        out_shape=jax.ShapeDtypeStruct((t, d), jnp.float32),
        scratch_shapes=[
            pltpu.VMEM((tm + 2 * CONV_HALO, d), jnp.float32),
            pltpu.VMEM((tm, d), jnp.float32),
        ],
        compiler_params=_params(1),
        name="conv_dw_pw2",
    )(u, u, u, x, mod, w_dw, b_dw, ln_g, ln_b, w_pw2, g1)


def _rope(x, cos, sin_signed):
    half = ROT_DIM // 2
    lane = lax.broadcasted_iota(jnp.int32, x.shape, 1)
    rot = jnp.where(lane < half,
                    pltpu.roll(x, HEAD_DIM - half, axis=1),
                    pltpu.roll(x, half, axis=1))
    return x * cos + rot * sin_signed


def _qkv_kernel(x_ref, mod_ref, g_ref, w_ref, cos_ref, sin_ref, q_ref, k_ref, vt_ref):
    nq = q_ref.shape[1]
    nk = k_ref.shape[1]
    h = _modulated_norm(x_ref[...], g_ref[...], mod_ref, 0, 1)
    cos = cos_ref[...]
    sin = sin_ref[...]
    chunk = 4 * HEAD_DIM
    for c in range(0, nq + nk, chunk):
        y = jnp.dot(h, w_ref[0, :, c:c + chunk], preferred_element_type=jnp.float32)
        for hd in range(0, chunk, HEAD_DIM):
            r = _rope(y[:, hd:hd + HEAD_DIM], cos, sin).astype(jnp.bfloat16)
            if c < nq:
                q_ref[:, c + hd:c + hd + HEAD_DIM] = r
            else:
                k_ref[:, c - nq + hd:c - nq + hd + HEAD_DIM] = r
    v = jnp.dot(h, w_ref[0, :, nq + nk:], preferred_element_type=jnp.float32)
    vt_ref[...] = v.T.astype(jnp.bfloat16)


def _qkv(x, mod, g0, w_qkv, layer, cos, sin, seq):
    t, d = x.shape
    tm = TOKEN_TILE
    nq = N_HEADS * HEAD_DIM
    nk = N_KV_HEADS * HEAD_DIM
    tiles_per_seq = seq // tm
    return pl.pallas_call(
        _qkv_kernel,
        grid=(t // tm,),
        in_specs=[
            pl.BlockSpec((tm, d), lambda j: (j, 0)),
            pl.BlockSpec((1, 6, d), lambda j: (j * tm // seq, 0, 0)),
            _resident((1, d)),
            _resident_layer((d, nq + 2 * nk), layer),
            pl.BlockSpec((tm, HEAD_DIM), lambda j: (j % tiles_per_seq, 0)),
            pl.BlockSpec((tm, HEAD_DIM), lambda j: (j % tiles_per_seq, 0)),
        ],
        out_specs=[
            pl.BlockSpec((tm, nq), lambda j: (j, 0)),
            pl.BlockSpec((tm, nk), lambda j: (j, 0)),
            pl.BlockSpec((nk, tm), lambda j: (0, j)),
        ],
        out_shape=[
            jax.ShapeDtypeStruct((t, nq), jnp.bfloat16),
            jax.ShapeDtypeStruct((t, nk), jnp.bfloat16),
            jax.ShapeDtypeStruct((nk, t), jnp.bfloat16),
        ],
        compiler_params=_params(1),
        name="attn_qkv_rope",
    )(x, mod, g0, w_qkv, cos, sin)


def _attn_kernel(seq, sink_ref, q_ref, k_ref, kp_ref, kn_ref, vt_ref, vtp_ref, vtn_ref,
                 x_ref, mod_ref, w_ref, g_ref, o_ref,
                 kext_ref, vtext_ref, oh_ref, bias_lo_ref, bias_hi_ref):
    tm = x_ref.shape[0]
    blk = WINDOW
    nblk = tm // blk
    cols = GROUP * blk
    j = pl.program_id(0)
    is_start = ((j * tm) % seq == 0).astype(jnp.int32)
    is_end = (((j + 1) * tm) % seq == 0).astype(jnp.int32)

    kext_ref[0:blk, :] = kp_ref[...]
    kext_ref[blk:blk + tm, :] = k_ref[...]
    kext_ref[blk + tm:, :] = kn_ref[...]
    vtext_ref[:, 0:blk] = vtp_ref[...]
    vtext_ref[:, blk:blk + tm] = vt_ref[...]
    vtext_ref[:, blk + tm:] = vtn_ref[...]

    kk = lax.broadcasted_iota(jnp.int32, (blk, cols), 0)
    qi = lax.broadcasted_iota(jnp.int32, (blk, cols), 1) & (blk - 1)
    bias_lo_ref[0] = jnp.where(kk >= qi, 0.0, NEG)
    bias_hi_ref[0] = jnp.where(kk <= qi, 0.0, NEG)
    bias_lo_ref[1] = jnp.full((blk, cols), NEG, jnp.float32)
    bias_hi_ref[1] = jnp.full((blk, cols), NEG, jnp.float32)
    scale = HEAD_DIM ** -0.5

    for b in range(nblk):
        lo_variant = is_start if b == 0 else 0
        hi_variant = is_end if b == nblk - 1 else 0
        for g in range(N_KV_HEADS):
            heads = range(g * GROUP, (g + 1) * GROUP)
            qs = jnp.concatenate(
                [q_ref[b * blk:(b + 1) * blk, h * HEAD_DIM:(h + 1) * HEAD_DIM] for h in heads],
                axis=0)
            kb = kext_ref[b * blk:(b + 3) * blk, g * HEAD_DIM:(g + 1) * HEAD_DIM]
            vtb = vtext_ref[g * HEAD_DIM:(g + 1) * HEAD_DIM, b * blk:(b + 3) * blk]
            sc = lax.dot_general(kb, qs, (((1,), (1,)), ((), ())),
                                 preferred_element_type=jnp.float32) * scale
            s_lo = sc[0:blk] + bias_lo_ref[lo_variant]
            s_mid = sc[blk:2 * blk]
            s_hi = sc[2 * blk:] + bias_hi_ref[hi_variant]
            sink = jnp.concatenate(
                [jnp.full((1, blk), sink_ref[h], jnp.float32) for h in heads], axis=1)
            m = jnp.max(jnp.maximum(jnp.maximum(s_lo, s_mid), s_hi), axis=0, keepdims=True)
            m = jnp.maximum(m, sink)
            p_lo = jnp.exp(s_lo - m)
            p_mid = jnp.exp(s_mid - m)
            p_hi = jnp.exp(s_hi - m)
            denom = jnp.sum(p_lo + p_mid + p_hi, axis=0, keepdims=True) + jnp.exp(sink - m)
            p = jnp.concatenate([p_lo, p_mid, p_hi], axis=0).astype(jnp.bfloat16)
            ot = jnp.dot(vtb, p, preferred_element_type=jnp.float32) / denom
            for n, h in enumerate(heads):
                oh_ref[b * blk:(b + 1) * blk, h * HEAD_DIM:(h + 1) * HEAD_DIM] = (
                    ot[:, n * blk:(n + 1) * blk].T.astype(jnp.bfloat16))

    out = jnp.dot(oh_ref[...], w_ref[0], preferred_element_type=jnp.float32)
    o_ref[...] = _gated_residual(x_ref[...], out, g_ref[...], mod_ref, 2)


def _attn(q, k, vt, x, mod, w_o, layer, g1, sink, seq):
    t, d = x.shape
    tm = TOKEN_TILE
    nq = q.shape[1]
    nk = k.shape[1]
    per = tm // WINDOW
    last = t // WINDOW - 1
    main = lambda j: (j, 0)
    prev = lambda j: (jnp.maximum(j * per - 1, 0), 0)
    nxt = lambda j: (jnp.minimum((j + 1) * per, last), 0)
    return pl.pallas_call(
        functools.partial(_attn_kernel, seq),
        grid=(t // tm,),
        in_specs=[
            pl.BlockSpec(memory_space=pltpu.SMEM),
            pl.BlockSpec((tm, nq), main),
            pl.BlockSpec((tm, nk), main),
            pl.BlockSpec((WINDOW, nk), prev),
            pl.BlockSpec((WINDOW, nk), nxt),
            pl.BlockSpec((nk, tm), lambda j: (0, j)),
            pl.BlockSpec((nk, WINDOW), lambda j: (0, jnp.maximum(j * per - 1, 0))),
            pl.BlockSpec((nk, WINDOW), lambda j: (0, jnp.minimum((j + 1) * per, last))),
            pl.BlockSpec((tm, d), main),
            pl.BlockSpec((1, 6, d), lambda j: (j * tm // seq, 0, 0)),
            _resident_layer((nq, d), layer),
            _resident((1, d)),
        ],
        out_specs=pl.BlockSpec((tm, d), main),
        out_shape=jax.ShapeDtypeStruct((t, d), jnp.float32),
        scratch_shapes=[
            pltpu.VMEM((tm + 2 * WINDOW, nk), jnp.bfloat16),
            pltpu.VMEM((nk, tm + 2 * WINDOW), jnp.bfloat16),
            pltpu.VMEM((tm, nq), jnp.bfloat16),
            pltpu.VMEM((2, WINDOW, GROUP * WINDOW), jnp.float32),
            pltpu.VMEM((2, WINDOW, GROUP * WINDOW), jnp.float32),
        ],
        compiler_params=_params(1),
        name="attn_band_out",
    )(sink, q, k, k, k, vt, vt, vt, x, mod, w_o, g1)


def _ffn_kernel(x_ref, mod_ref, g2_ref, g3_ref, wg_ref, wu_ref, wd_ref, o_ref, h_ref):
    f = pl.program_id(1)

    @pl.when(f == 0)
    def _():
        h_ref[...] = _modulated_norm(x_ref[...], g2_ref[...], mod_ref, 3, 4)
        o_ref[...] = jnp.zeros_like(o_ref)

    h = h_ref[...]
    g = jnp.dot(h, wg_ref[0], preferred_element_type=jnp.float32)
    u = jnp.dot(h, wu_ref[0], preferred_element_type=jnp.float32)
    a = (g * _sigmoid(g) * u).astype(jnp.bfloat16)
    o_ref[...] += jnp.dot(a, wd_ref[0], preferred_element_type=jnp.float32)

    @pl.when(f == pl.num_programs(1) - 1)
    def _():
        o_ref[...] = _gated_residual(x_ref[...], o_ref[...], g3_ref[...], mod_ref, 5)


def _ffn(x, mod, g2, g3, w_gate, w_up, w_down, layer, seq):
    t, d = x.shape
    tm = FFN_TOKEN_TILE
    tf = FF_TILE
    nf = w_gate.shape[2] // tf
    return pl.pallas_call(
        _ffn_kernel,
        grid=(t // tm, nf),
        in_specs=[
            pl.BlockSpec((tm, d), lambda j, f: (j, 0)),
            pl.BlockSpec((1, 6, d), lambda j, f: (j * tm // seq, 0, 0)),
            _resident((1, d)),
            _resident((1, d)),
            pl.BlockSpec((1, d, tf), lambda j, f: (layer, 0, f)),
            pl.BlockSpec((1, d, tf), lambda j, f: (layer, 0, f)),
            pl.BlockSpec((1, tf, d), lambda j, f: (layer, f, 0)),
        ],
        out_specs=pl.BlockSpec((tm, d), lambda j, f: (j, 0)),
        out_shape=jax.ShapeDtypeStruct((t, d), jnp.float32),
        scratch_shapes=[pltpu.VMEM((tm, d), jnp.bfloat16)],
        compiler_params=_params(2),
        name="ffn_swiglu",
    )(x, mod, g2, g3, w_gate, w_up, w_down)


def _rope_tables(seq):
    inv_freq = ROPE_THETA ** (-jnp.arange(0, ROT_DIM, 2, dtype=jnp.float32) / ROT_DIM)
    ang = jnp.arange(seq, dtype=jnp.float32)[:, None] * inv_freq[None, :]
    pad = HEAD_DIM - ROT_DIM
    cos = jnp.concatenate([jnp.cos(ang), jnp.cos(ang), jnp.ones((seq, pad), jnp.float32)], -1)
    sin = jnp.concatenate([-jnp.sin(ang), jnp.sin(ang), jnp.zeros((seq, pad), jnp.float32)], -1)
    return cos, sin


def _trunk(x, mods, seq, weights):
    (norm_g, w_gate, w_up, w_down, conv_w_pw1, conv_w_dw, conv_b_dw, conv_ln_g, conv_ln_b,
     conv_w_pw2, attn_w_qkv, attn_w_o, attn_sink) = weights
    depth = norm_g.shape[0]
    d = x.shape[1]
    cos, sin = _rope_tables(seq)
    for i in range(depth):
        mod = mods[i]
        g = [norm_g[i, n].reshape(1, d) for n in range(4)]
        j = i // 2
        if i % 2 == 0:
            u = _conv1(x, mod, g[0], conv_w_pw1, j, seq)
            x = _conv2(u, x, mod, conv_w_dw[j], conv_b_dw[j].reshape(1, d),
                       conv_ln_g[j].reshape(1, d), conv_ln_b[j].reshape(1, d),
                       conv_w_pw2, j, g[1], seq)
        else:
            q, k, vt = _qkv(x, mod, g[0], attn_w_qkv, j, cos, sin, seq)
            x = _attn(q, k, vt, x, mod, attn_w_o, j, g[1], attn_sink[j], seq)
        x = _ffn(x, mod, g[2], g[3], w_gate, w_up, w_down, i, seq)
    return x


def kernel(x_prompt, x_sample, c_prompt, c_sample, w_ada, b_ada, norm_g, w_gate, w_up, w_down,
           conv_w_pw1, conv_w_dw, conv_b_dw, conv_ln_g, conv_ln_b, conv_w_pw2,
           attn_w_q, attn_w_k, attn_w_v, attn_w_o, attn_sink):
    bf16 = jnp.bfloat16
    depth, d, d_ff = w_gate.shape
    bp, sp, _ = x_prompt.shape
    bs, ss, _ = x_sample.shape
    assert sp % FFN_TOKEN_TILE == 0 and ss % FFN_TOKEN_TILE == 0 and d_ff % FF_TILE == 0
    assert FFN_TOKEN_TILE % TOKEN_TILE == 0 and TOKEN_TILE % WINDOW == 0
    assert TOKEN_TILE % CONV_ROWS == 0 and d % CONV_COLS == 0 and (6 * d) % ADA_TILE == 0

    n_rows = bp + bs
    rows = -(-n_rows // 8) * 8
    c_all = jnp.concatenate(
        [c_prompt, c_sample, jnp.zeros((rows - n_rows, d), jnp.float32)], axis=0)
    mod_all = _ada(c_all, w_ada, b_ada).reshape(depth, rows, 6, d)
    mods_p = [mod_all[i, :bp] for i in range(depth)]
    mods_s = [mod_all[i, bp:n_rows] for i in range(depth)]

    w_qkv = jnp.concatenate([attn_w_q, attn_w_k, attn_w_v], axis=-1).astype(bf16)
    weights = (norm_g, w_gate.astype(bf16), w_up.astype(bf16), w_down.astype(bf16),
               conv_w_pw1.astype(bf16), conv_w_dw, conv_b_dw,
               conv_ln_g, conv_ln_b, conv_w_pw2.astype(bf16), w_qkv,
               attn_w_o.astype(bf16), attn_sink)

    y_p = _trunk(x_prompt.reshape(bp * sp, d), mods_p, sp, weights)
    y_s = _trunk(x_sample.reshape(bs * ss, d), mods_s, ss, weights)
    return y_p.reshape(bp, sp, d), y_s.reshape(bs, ss, d)
```

```python
import functools

import jax
import jax.numpy as jnp
from jax import lax
from jax.experimental import pallas as pl
from jax.experimental.pallas import tpu as pltpu

N_HEADS = 16
N_KV_HEADS = 4
GROUP = N_HEADS // N_KV_HEADS
HEAD_DIM = 128
ROT_DIM = HEAD_DIM // 4
ROPE_THETA = 500000.0
WINDOW = 128
CONV_WIDTH = 31
CONV_HALO = 16
SUBLANES = 8
ROW_CHUNK = 16
CONV_ROWS = 64
CONV_COLS = 128
EPS = 1e-6
NEG = -1e30

TOKEN_TILE = 512
FFN_TOKEN_TILE = 512
FF_TILE = 512
ADA_TILE = 1536
VMEM_LIMIT_BYTES = 56 * 1024 * 1024


def _params(n_axes):
    return pltpu.CompilerParams(
        dimension_semantics=("arbitrary",) * n_axes,
        vmem_limit_bytes=VMEM_LIMIT_BYTES)


def _resident(shape):
    return pl.BlockSpec(shape, lambda *_: (0,) * len(shape),
                        pipeline_mode=pl.Buffered(1))


def _resident_layer(shape, layer):
    return pl.BlockSpec((1,) + shape, lambda *_: (layer,) + (0,) * len(shape),
                        pipeline_mode=pl.Buffered(1))


def _sigmoid(x):
    return 1.0 / (1.0 + jnp.exp(-x))


def _rms(xf, g):
    return xf * lax.rsqrt(jnp.mean(xf * xf, axis=-1, keepdims=True) + EPS) * g


def _row_chunks(n_rows):
    return range(0, n_rows, ROW_CHUNK)


def _modulated_norm(x, g, mod_ref, shift_row, scale_row):
    shift = mod_ref[0, shift_row:shift_row + 1, :]
    scale = mod_ref[0, scale_row:scale_row + 1, :]
    return (_rms(x, g) * (1.0 + scale) + shift).astype(jnp.bfloat16)


def _modulated_norm_rows(h_ref, x_ref, g, mod_ref, shift_row, scale_row):
    shift = mod_ref[0, shift_row:shift_row + 1, :]
    scale1 = 1.0 + mod_ref[0, scale_row:scale_row + 1, :]
    for r in _row_chunks(x_ref.shape[0]):
        x = x_ref[r:r + ROW_CHUNK, :]
        h_ref[r:r + ROW_CHUNK, :] = (_rms(x, g) * scale1 + shift).astype(jnp.bfloat16)


def _gated_residual(o_ref, x_ref, g, mod_ref, gate_row):
    gate = mod_ref[0, gate_row:gate_row + 1, :]
    for r in _row_chunks(x_ref.shape[0]):
        out = o_ref[r:r + ROW_CHUNK, :]
        o_ref[r:r + ROW_CHUNK, :] = x_ref[r:r + ROW_CHUNK, :] + gate * _rms(out, g)


def _ada_kernel(c_ref, w_ref, b_ref, o_ref):
    c = c_ref[...]
    ca = (c * _sigmoid(c)).astype(jnp.bfloat16)
    w = w_ref[0].astype(jnp.bfloat16)
    o_ref[0] = jnp.dot(ca, w, preferred_element_type=jnp.float32) + b_ref[0]


def _ada(c_all, w_ada, b_ada):
    depth, d, n = w_ada.shape
    rows = c_all.shape[0]
    return pl.pallas_call(
        _ada_kernel,
        grid=(depth, n // ADA_TILE),
        in_specs=[
            pl.BlockSpec((rows, d), lambda i, j: (0, 0)),
            pl.BlockSpec((1, d, ADA_TILE), lambda i, j: (i, 0, j)),
            pl.BlockSpec((1, 1, ADA_TILE), lambda i, j: (i, 0, j)),
        ],
        out_specs=pl.BlockSpec((1, rows, ADA_TILE), lambda i, j: (i, 0, j)),
        out_shape=jax.ShapeDtypeStruct((depth, rows, n), jnp.float32),
        compiler_params=_params(2),
        name="ada_mod",
    )(c_all, w_ada, b_ada.reshape(depth, 1, n))


def _conv1_kernel(x_ref, mod_ref, g_ref, w_ref, u_ref):
    d = x_ref.shape[1]
    h = _modulated_norm(x_ref[...], g_ref[...], mod_ref, 0, 1)
    chunk = 512
    for c in range(0, d, chunk):
        a = jnp.dot(h, w_ref[0, :, c:c + chunk], preferred_element_type=jnp.float32)
        g = jnp.dot(h, w_ref[0, :, d + c:d + c + chunk], preferred_element_type=jnp.float32)
        u_ref[:, c:c + chunk] = a * _sigmoid(g)


def _conv1(x, mod, g0, w_pw1, layer, seq):
    t, d = x.shape
    tm = TOKEN_TILE
    return pl.pallas_call(
        _conv1_kernel,
        grid=(t // tm,),
        in_specs=[
            pl.BlockSpec((tm, d), lambda j: (j, 0)),
            pl.BlockSpec((1, 6, d), lambda j: (j * tm // seq, 0, 0)),
            _resident((1, d)),
            _resident_layer((d, 2 * d), layer),
        ],
        out_specs=pl.BlockSpec((tm, d), lambda j: (j, 0)),
        out_shape=jax.ShapeDtypeStruct((t, d), jnp.float32),
        compiler_params=_params(1),
        name="conv_pw1_glu",
    )(x, mod, g0, w_pw1)


def _conv2_kernel(seq, u_ref, up_ref, un_ref, x_ref, mod_ref, wdw_ref, bdw_ref,
                  lng_ref, lnb_ref, w_ref, g_ref, o_ref, ext_ref, y_ref):
    tm, d = x_ref.shape
    j = pl.program_id(0)
    is_start = (j * tm) % seq == 0
    is_end = ((j + 1) * tm) % seq == 0
    ext_ref[0:CONV_HALO, :] = jnp.where(is_start, 0.0, up_ref[...])
    ext_ref[CONV_HALO:CONV_HALO + tm, :] = u_ref[...]
    ext_ref[CONV_HALO + tm:, :] = jnp.where(is_end, 0.0, un_ref[...])

    rows, cols = CONV_ROWS, CONV_COLS
    first = CONV_HALO - CONV_WIDTH // 2

    def row_block(r, carry):
        r0 = pl.multiple_of(r * rows, rows)
        for c in range(0, d, cols):
            n_win = (first + CONV_WIDTH - 1) // SUBLANES + 1
            wins = [ext_ref[pl.ds(r0 + SUBLANES * a, rows + SUBLANES), c:c + cols]
                    for a in range(n_win)]
            acc = None
            for b in range(SUBLANES):
                part = None
                for a in range((first + CONV_WIDTH - 1 - b) // SUBLANES + 1):
                    w = SUBLANES * a + b - first
                    if w < 0:
                        continue
                    term = wins[a] * wdw_ref[w:w + 1, c:c + cols]
                    part = term if part is None else part + term
                part = part[b:b + rows]
                acc = part if acc is None else acc + part
            y_ref[pl.ds(r0, rows), c:c + cols] = acc + bdw_ref[:, c:c + cols]
        return carry

    lax.fori_loop(0, tm // rows, row_block, 0)

    y = y_ref[...]
    mu = jnp.mean(y, axis=-1, keepdims=True)
    yc = y - mu
    z = yc * lax.rsqrt(jnp.mean(yc * yc, axis=-1, keepdims=True) + EPS)
    z = z * lng_ref[...] + lnb_ref[...]
    s = (z * _sigmoid(z)).astype(jnp.bfloat16)
    o_ref[...] = jnp.dot(s, w_ref[0], preferred_element_type=jnp.float32)
    _gated_residual(o_ref, x_ref, g_ref[...], mod_ref, 2)


def _conv2(u, x, mod, w_dw, b_dw, ln_g, ln_b, w_pw2, layer, g1, seq):
    t, d = x.shape
    tm = TOKEN_TILE
    per = tm // CONV_HALO
    last = t // CONV_HALO - 1
    return pl.pallas_call(
        functools.partial(_conv2_kernel, seq),
        grid=(t // tm,),
        in_specs=[
            pl.BlockSpec((tm, d), lambda j: (j, 0)),
            pl.BlockSpec((CONV_HALO, d), lambda j: (jnp.maximum(j * per - 1, 0), 0)),
            pl.BlockSpec((CONV_HALO, d), lambda j: (jnp.minimum((j + 1) * per, last), 0)),
            pl.BlockSpec((tm, d), lambda j: (j, 0)),
            pl.BlockSpec((1, 6, d), lambda j: (j * tm // seq, 0, 0)),
            _resident((CONV_WIDTH, d)),
            _resident((1, d)),
            _resident((1, d)),
            _resident((1, d)),
            _resident_layer((d, d), layer),
            _resident((1, d)),
        ],
        out_specs=pl.BlockSpec((tm, d), lambda j: (j, 0)),
        out_shape=jax.ShapeDtypeStruct((t, d), jnp.float32),
        scratch_shapes=[
            pltpu.VMEM((tm + 2 * CONV_HALO, d), jnp.float32),
            pltpu.VMEM((tm, d), jnp.float32),
        ],
        compiler_params=_params(1),
        name="conv_dw_pw2",
    )(u, u, u, x, mod, w_dw, b_dw, ln_g, ln_b, w_pw2, g1)


def _rope(x, cos, sin_signed):
    half = ROT_DIM // 2
    lane = lax.broadcasted_iota(jnp.int32, x.shape, 1)
    rot = jnp.where(lane < half,
                    pltpu.roll(x, HEAD_DIM - half, axis=1),
                    pltpu.roll(x, half, axis=1))
    return x * cos + rot * sin_signed


def _qkv_kernel(x_ref, mod_ref, g_ref, w_ref, cos_ref, sin_ref, q_ref, k_ref, vt_ref):
    nq = q_ref.shape[1]
    nk = k_ref.shape[1]
    h = _modulated_norm(x_ref[...], g_ref[...], mod_ref, 0, 1)
    cos = cos_ref[...]
    sin = sin_ref[...]
    chunk = 4 * HEAD_DIM
    for c in range(0, nq + nk, chunk):
        y = jnp.dot(h, w_ref[0, :, c:c + chunk], preferred_element_type=jnp.float32)
        for hd in range(0, chunk, HEAD_DIM):
            r = _rope(y[:, hd:hd + HEAD_DIM], cos, sin).astype(jnp.bfloat16)
            if c < nq:
                q_ref[:, c + hd:c + hd + HEAD_DIM] = r
            else:
                k_ref[:, c - nq + hd:c - nq + hd + HEAD_DIM] = r
    v = jnp.dot(h, w_ref[0, :, nq + nk:], preferred_element_type=jnp.float32)
    vt_ref[...] = v.T.astype(jnp.bfloat16)


def _qkv(x, mod, g0, w_qkv, layer, cos, sin, seq):
    t, d = x.shape
    tm = TOKEN_TILE
    nq = N_HEADS * HEAD_DIM
    nk = N_KV_HEADS * HEAD_DIM
    tiles_per_seq = seq // tm
    return pl.pallas_call(
        _qkv_kernel,
        grid=(t // tm,),
        in_specs=[
            pl.BlockSpec((tm, d), lambda j: (j, 0)),
            pl.BlockSpec((1, 6, d), lambda j: (j * tm // seq, 0, 0)),
            _resident((1, d)),
            _resident_layer((d, nq + 2 * nk), layer),
            pl.BlockSpec((tm, HEAD_DIM), lambda j: (j % tiles_per_seq, 0)),
            pl.BlockSpec((tm, HEAD_DIM), lambda j: (j % tiles_per_seq, 0)),
        ],
        out_specs=[
            pl.BlockSpec((tm, nq), lambda j: (j, 0)),
            pl.BlockSpec((tm, nk), lambda j: (j, 0)),
            pl.BlockSpec((nk, tm), lambda j: (0, j)),
        ],
        out_shape=[
            jax.ShapeDtypeStruct((t, nq), jnp.bfloat16),
            jax.ShapeDtypeStruct((t, nk), jnp.bfloat16),
            jax.ShapeDtypeStruct((nk, t), jnp.bfloat16),
        ],
        compiler_params=_params(1),
        name="attn_qkv_rope",
    )(x, mod, g0, w_qkv, cos, sin)


def _attn_kernel(seq, sink_ref, q_ref, k_ref, kp_ref, kn_ref, vt_ref, vtp_ref, vtn_ref,
                 x_ref, mod_ref, w_ref, g_ref, o_ref,
                 kext_ref, vtext_ref, oh_ref, bias_lo_ref, bias_hi_ref):
    tm = x_ref.shape[0]
    blk = WINDOW
    nblk = tm // blk
    cols = GROUP * blk
    j = pl.program_id(0)
    is_start = ((j * tm) % seq == 0).astype(jnp.int32)
    is_end = (((j + 1) * tm) % seq == 0).astype(jnp.int32)

    kext_ref[0:blk, :] = kp_ref[...]
    kext_ref[blk:blk + tm, :] = k_ref[...]
    kext_ref[blk + tm:, :] = kn_ref[...]
    vtext_ref[:, 0:blk] = vtp_ref[...]
    vtext_ref[:, blk:blk + tm] = vt_ref[...]
    vtext_ref[:, blk + tm:] = vtn_ref[...]

    kk = lax.broadcasted_iota(jnp.int32, (blk, cols), 0)
    qi = lax.broadcasted_iota(jnp.int32, (blk, cols), 1) & (blk - 1)
    bias_lo_ref[0] = jnp.where(kk >= qi, 0.0, NEG)
    bias_hi_ref[0] = jnp.where(kk <= qi, 0.0, NEG)
    bias_lo_ref[1] = jnp.full((blk, cols), NEG, jnp.float32)
    bias_hi_ref[1] = jnp.full((blk, cols), NEG, jnp.float32)
    scale = HEAD_DIM ** -0.5

    for b in range(nblk):
        lo_variant = is_start if b == 0 else 0
        hi_variant = is_end if b == nblk - 1 else 0
        for g in range(N_KV_HEADS):
            heads = range(g * GROUP, (g + 1) * GROUP)
            qs = jnp.concatenate(
                [q_ref[b * blk:(b + 1) * blk, h * HEAD_DIM:(h + 1) * HEAD_DIM] for h in heads],
                axis=0)
            kb = kext_ref[b * blk:(b + 3) * blk, g * HEAD_DIM:(g + 1) * HEAD_DIM]
            vtb = vtext_ref[g * HEAD_DIM:(g + 1) * HEAD_DIM, b * blk:(b + 3) * blk]
            sc = lax.dot_general(kb, qs, (((1,), (1,)), ((), ())),
                                 preferred_element_type=jnp.float32) * scale
            s_lo = sc[0:blk] + bias_lo_ref[lo_variant]
            s_mid = sc[blk:2 * blk]
            s_hi = sc[2 * blk:] + bias_hi_ref[hi_variant]
            sink = jnp.concatenate(
                [jnp.full((1, blk), sink_ref[h], jnp.float32) for h in heads], axis=1)
            m = jnp.max(jnp.maximum(jnp.maximum(s_lo, s_mid), s_hi), axis=0, keepdims=True)
            m = jnp.maximum(m, sink)
            p_lo = jnp.exp(s_lo - m)
            p_mid = jnp.exp(s_mid - m)
            p_hi = jnp.exp(s_hi - m)
            denom = jnp.sum(p_lo + p_mid + p_hi, axis=0, keepdims=True) + jnp.exp(sink - m)
            p = jnp.concatenate([p_lo, p_mid, p_hi], axis=0).astype(jnp.bfloat16)
            ot = jnp.dot(vtb, p, preferred_element_type=jnp.float32) / denom
            for n, h in enumerate(heads):
                oh_ref[b * blk:(b + 1) * blk, h * HEAD_DIM:(h + 1) * HEAD_DIM] = (
                    ot[:, n * blk:(n + 1) * blk].T.astype(jnp.bfloat16))

    o_ref[...] = jnp.dot(oh_ref[...], w_ref[0], preferred_element_type=jnp.float32)
    _gated_residual(o_ref, x_ref, g_ref[...], mod_ref, 2)


def _attn(q, k, vt, x, mod, w_o, layer, g1, sink, seq):
    t, d = x.shape
    tm = TOKEN_TILE
    nq = q.shape[1]
    nk = k.shape[1]
    per = tm // WINDOW
    last = t // WINDOW - 1
    main = lambda j: (j, 0)
    prev = lambda j: (jnp.maximum(j * per - 1, 0), 0)
    nxt = lambda j: (jnp.minimum((j + 1) * per, last), 0)
    return pl.pallas_call(
        functools.partial(_attn_kernel, seq),
        grid=(t // tm,),
        in_specs=[
            pl.BlockSpec(memory_space=pltpu.SMEM),
            pl.BlockSpec((tm, nq), main),
            pl.BlockSpec((tm, nk), main),
            pl.BlockSpec((WINDOW, nk), prev),
            pl.BlockSpec((WINDOW, nk), nxt),
            pl.BlockSpec((nk, tm), lambda j: (0, j)),
            pl.BlockSpec((nk, WINDOW), lambda j: (0, jnp.maximum(j * per - 1, 0))),
            pl.BlockSpec((nk, WINDOW), lambda j: (0, jnp.minimum((j + 1) * per, last))),
            pl.BlockSpec((tm, d), main),
            pl.BlockSpec((1, 6, d), lambda j: (j * tm // seq, 0, 0)),
            _resident_layer((nq, d), layer),
            _resident((1, d)),
        ],
        out_specs=pl.BlockSpec((tm, d), main),
        out_shape=jax.ShapeDtypeStruct((t, d), jnp.float32),
        scratch_shapes=[
            pltpu.VMEM((tm + 2 * WINDOW, nk), jnp.bfloat16),
            pltpu.VMEM((nk, tm + 2 * WINDOW), jnp.bfloat16),
            pltpu.VMEM((tm, nq), jnp.bfloat16),
            pltpu.VMEM((2, WINDOW, GROUP * WINDOW), jnp.float32),
            pltpu.VMEM((2, WINDOW, GROUP * WINDOW), jnp.float32),
        ],
        compiler_params=_params(1),
        name="attn_band_out",
    )(sink, q, k, k, k, vt, vt, vt, x, mod, w_o, g1)


def _ffn_kernel(layer, x_ref, mod_ref, g2_ref, g3_ref, wg_hbm, wu_hbm, wd_hbm, o_ref,
                h_ref, wg_buf, wu_buf, wd_buf, sem):
    j = pl.program_id(0)
    tf = wg_buf.shape[2]
    nf = wg_hbm.shape[2] // tf
    total = pl.num_programs(0) * nf

    def chunk_copies(f, slot):
        cols = pl.ds(pl.multiple_of(f * tf, tf), tf)
        return (
            pltpu.make_async_copy(wg_hbm.at[layer, :, cols], wg_buf.at[slot], sem.at[0, slot]),
            pltpu.make_async_copy(wu_hbm.at[layer, :, cols], wu_buf.at[slot], sem.at[1, slot]),
            pltpu.make_async_copy(wd_hbm.at[layer, cols, :], wd_buf.at[slot], sem.at[2, slot]),
        )

    @pl.when(j == 0)
    def _():
        for c in range(2):
            for cp in chunk_copies(c, c):
                cp.start()

    _modulated_norm_rows(h_ref, x_ref, g2_ref[...], mod_ref, 3, 4)
    o_ref[...] = jnp.zeros_like(o_ref)

    def chunk(f, carry):
        c = j * nf + f
        slot = c & 1
        for cp in chunk_copies(f, slot):
            cp.wait()
        h = h_ref[...]
        g = jnp.dot(h, wg_buf[slot], preferred_element_type=jnp.float32)
        u = jnp.dot(h, wu_buf[slot], preferred_element_type=jnp.float32)
        a = (g * _sigmoid(g) * u).astype(jnp.bfloat16)
        o_ref[...] += jnp.dot(a, wd_buf[slot], preferred_element_type=jnp.float32)

        @pl.when(c + 2 < total)
        def _():
            nxt = jnp.where(f + 2 >= nf, f + 2 - nf, f + 2)
            for cp in chunk_copies(nxt, slot):
                cp.start()

        return carry

    lax.fori_loop(0, nf, chunk, 0)
    _gated_residual(o_ref, x_ref, g3_ref[...], mod_ref, 5)


def _ffn(x, mod, g2, g3, w_gate, w_up, w_down, layer, seq):
    t, d = x.shape
    tm = FFN_TOKEN_TILE
    tf = FF_TILE
    assert w_gate.shape[2] // tf >= 2
    return pl.pallas_call(
        functools.partial(_ffn_kernel, layer),
        grid=(t // tm,),
        in_specs=[
            pl.BlockSpec((tm, d), lambda j: (j, 0)),
            pl.BlockSpec((1, 6, d), lambda j: (j * tm // seq, 0, 0)),
            _resident((1, d)),
            _resident((1, d)),
            pl.BlockSpec(memory_space=pl.ANY),
            pl.BlockSpec(memory_space=pl.ANY),
            pl.BlockSpec(memory_space=pl.ANY),
        ],
        out_specs=pl.BlockSpec((tm, d), lambda j: (j, 0)),
        out_shape=jax.ShapeDtypeStruct((t, d), jnp.float32),
        scratch_shapes=[
            pltpu.VMEM((tm, d), jnp.bfloat16),
            pltpu.VMEM((2, d, tf), jnp.bfloat16),
            pltpu.VMEM((2, d, tf), jnp.bfloat16),
            pltpu.VMEM((2, tf, d), jnp.bfloat16),
            pltpu.SemaphoreType.DMA((3, 2)),
        ],
        compiler_params=_params(1),
        name="ffn_swiglu",
    )(x, mod, g2, g3, w_gate, w_up, w_down)


def _rope_tables(seq):
    inv_freq = ROPE_THETA ** (-jnp.arange(0, ROT_DIM, 2, dtype=jnp.float32) / ROT_DIM)
    ang = jnp.arange(seq, dtype=jnp.float32)[:, None] * inv_freq[None, :]
    pad = HEAD_DIM - ROT_DIM
    cos = jnp.concatenate([jnp.cos(ang), jnp.cos(ang), jnp.ones((seq, pad), jnp.float32)], -1)
    sin = jnp.concatenate([-jnp.sin(ang), jnp.sin(ang), jnp.zeros((seq, pad), jnp.float32)], -1)
    return cos, sin


def _trunk(x, mods, seq, weights):
    (norm_g, w_gate, w_up, w_down, conv_w_pw1, conv_w_dw, conv_b_dw, conv_ln_g, conv_ln_b,
     conv_w_pw2, attn_w_qkv, attn_w_o, attn_sink) = weights
    depth = norm_g.shape[0]
    d = x.shape[1]
    cos, sin = _rope_tables(seq)
    for i in range(depth):
        mod = mods[i]
        g = [norm_g[i, n].reshape(1, d) for n in range(4)]
        j = i // 2
        if i % 2 == 0:
            u = _conv1(x, mod, g[0], conv_w_pw1, j, seq)
            x = _conv2(u, x, mod, conv_w_dw[j], conv_b_dw[j].reshape(1, d),
                       conv_ln_g[j].reshape(1, d), conv_ln_b[j].reshape(1, d),
                       conv_w_pw2, j, g[1], seq)
        else:
            q, k, vt = _qkv(x, mod, g[0], attn_w_qkv, j, cos, sin, seq)
            x = _attn(q, k, vt, x, mod, attn_w_o, j, g[1], attn_sink[j], seq)
        x = _ffn(x, mod, g[2], g[3], w_gate, w_up, w_down, i, seq)
    return x


def kernel(x_prompt, x_sample, c_prompt, c_sample, w_ada, b_ada, norm_g, w_gate, w_up, w_down,
           conv_w_pw1, conv_w_dw, conv_b_dw, conv_ln_g, conv_ln_b, conv_w_pw2,
           attn_w_q, attn_w_k, attn_w_v, attn_w_o, attn_sink):
    bf16 = jnp.bfloat16
    depth, d, d_ff = w_gate.shape
    bp, sp, _ = x_prompt.shape
    bs, ss, _ = x_sample.shape
    assert sp % FFN_TOKEN_TILE == 0 and ss % FFN_TOKEN_TILE == 0 and d_ff % FF_TILE == 0
    assert FFN_TOKEN_TILE % TOKEN_TILE == 0 and TOKEN_TILE % WINDOW == 0
    assert TOKEN_TILE % CONV_ROWS == 0 and d % CONV_COLS == 0 and (6 * d) % ADA_TILE == 0

    n_rows = bp + bs
    rows = -(-n_rows // 8) * 8
    c_all = jnp.concatenate(
        [c_prompt, c_sample, jnp.zeros((rows - n_rows, d), jnp.float32)], axis=0)
    mod_all = _ada(c_all, w_ada, b_ada).reshape(depth, rows, 6, d)
    mods_p = [mod_all[i, :bp] for i in range(depth)]
    mods_s = [mod_all[i, bp:n_rows] for i in range(depth)]

    w_qkv = jnp.concatenate([attn_w_q, attn_w_k, attn_w_v], axis=-1).astype(bf16)
    weights = (norm_g, w_gate.astype(bf16), w_up.astype(bf16), w_down.astype(bf16),
               conv_w_pw1.astype(bf16), conv_w_dw, conv_b_dw,
               conv_ln_g, conv_ln_b, conv_w_pw2.astype(bf16), w_qkv,
               attn_w_o.astype(bf16), attn_sink)

    y_p = _trunk(x_prompt.reshape(bp * sp, d), mods_p, sp, weights)
    y_s = _trunk(x_sample.reshape(bs * ss, d), mods_s, ss, weights)
    return y_p.reshape(bp, sp, d), y_s.reshape(bs, ss, d)
```

```python
import functools

import jax
import jax.numpy as jnp
from jax import lax
from jax.experimental import pallas as pl
from jax.experimental.pallas import tpu as pltpu

N_HEADS = 16
N_KV_HEADS = 4
GROUP = N_HEADS // N_KV_HEADS
HEAD_DIM = 128
ROT_DIM = HEAD_DIM // 4
ROPE_THETA = 500000.0
WINDOW = 128
CONV_WIDTH = 31
CONV_HALO = 16
SUBLANES = 8
ROW_CHUNK = 16
CONV_ROWS = 64
CONV_COLS = 128
EPS = 1e-6
NEG = -1e30

TOKEN_TILE = 512
FFN_TOKEN_TILE = 512
FF_TILE = 512
FFN_SLOTS = 3
ADA_TILE = 1536
VMEM_LIMIT_BYTES = 56 * 1024 * 1024


def _params(n_axes):
    return pltpu.CompilerParams(
        dimension_semantics=("arbitrary",) * n_axes,
        vmem_limit_bytes=VMEM_LIMIT_BYTES)


def _resident(shape):
    return pl.BlockSpec(shape, lambda *_: (0,) * len(shape),
                        pipeline_mode=pl.Buffered(1))


def _resident_layer(shape, layer):
    return pl.BlockSpec((1,) + shape, lambda *_: (layer,) + (0,) * len(shape),
                        pipeline_mode=pl.Buffered(1))


def _sigmoid(x):
    return 1.0 / (1.0 + jnp.exp(-x))


def _rms(xf, g):
    return xf * lax.rsqrt(jnp.mean(xf * xf, axis=-1, keepdims=True) + EPS) * g


def _row_chunks(n_rows):
    return range(0, n_rows, ROW_CHUNK)


def _modulated_norm(x, g, mod_ref, shift_row, scale_row):
    shift = mod_ref[0, shift_row:shift_row + 1, :]
    scale = mod_ref[0, scale_row:scale_row + 1, :]
    return (_rms(x, g) * (1.0 + scale) + shift).astype(jnp.bfloat16)


def _modulated_norm_rows(h_ref, x_ref, g, mod_ref, shift_row, scale_row):
    shift = mod_ref[0, shift_row:shift_row + 1, :]
    scale1 = 1.0 + mod_ref[0, scale_row:scale_row + 1, :]
    for r in _row_chunks(x_ref.shape[0]):
        x = x_ref[r:r + ROW_CHUNK, :]
        h_ref[r:r + ROW_CHUNK, :] = (_rms(x, g) * scale1 + shift).astype(jnp.bfloat16)


def _gated_residual(o_ref, x_ref, g, mod_ref, gate_row, first_row=0, n_rows=None):
    gate = mod_ref[0, gate_row:gate_row + 1, :]
    n_rows = x_ref.shape[0] if n_rows is None else n_rows
    for r in range(first_row, first_row + n_rows, ROW_CHUNK):
        out = o_ref[r:r + ROW_CHUNK, :]
        o_ref[r:r + ROW_CHUNK, :] = x_ref[r:r + ROW_CHUNK, :] + gate * _rms(out, g)


def _ada_kernel(c_ref, w_ref, b_ref, o_ref):
    c = c_ref[...]
    ca = (c * _sigmoid(c)).astype(jnp.bfloat16)
    w = w_ref[0].astype(jnp.bfloat16)
    o_ref[0] = jnp.dot(ca, w, preferred_element_type=jnp.float32) + b_ref[0]


def _ada(c_all, w_ada, b_ada):
    depth, d, n = w_ada.shape
    rows = c_all.shape[0]
    return pl.pallas_call(
        _ada_kernel,
        grid=(depth, n // ADA_TILE),
        in_specs=[
            pl.BlockSpec((rows, d), lambda i, j: (0, 0)),
            pl.BlockSpec((1, d, ADA_TILE), lambda i, j: (i, 0, j)),
            pl.BlockSpec((1, 1, ADA_TILE), lambda i, j: (i, 0, j)),
        ],
        out_specs=pl.BlockSpec((1, rows, ADA_TILE), lambda i, j: (i, 0, j)),
        out_shape=jax.ShapeDtypeStruct((depth, rows, n), jnp.float32),
        compiler_params=_params(2),
        name="ada_mod",
    )(c_all, w_ada, b_ada.reshape(depth, 1, n))


def _conv1_kernel(x_ref, mod_ref, g_ref, w_ref, u_ref):
    d = x_ref.shape[1]
    h = _modulated_norm(x_ref[...], g_ref[...], mod_ref, 0, 1)
    chunk = 512
    for c in range(0, d, chunk):
        a = jnp.dot(h, w_ref[0, :, c:c + chunk], preferred_element_type=jnp.float32)
        g = jnp.dot(h, w_ref[0, :, d + c:d + c + chunk], preferred_element_type=jnp.float32)
        u_ref[:, c:c + chunk] = a * _sigmoid(g)


def _conv1(x, mod, g0, w_pw1, layer, seq):
    t, d = x.shape
    tm = TOKEN_TILE
    return pl.pallas_call(
        _conv1_kernel,
        grid=(t // tm,),
        in_specs=[
            pl.BlockSpec((tm, d), lambda j: (j, 0)),
            pl.BlockSpec((1, 6, d), lambda j: (j * tm // seq, 0, 0)),
            _resident((1, d)),
            _resident_layer((d, 2 * d), layer),
        ],
        out_specs=pl.BlockSpec((tm, d), lambda j: (j, 0)),
        out_shape=jax.ShapeDtypeStruct((t, d), jnp.float32),
        compiler_params=_params(1),
        name="conv_pw1_glu",
    )(x, mod, g0, w_pw1)


def _conv2_kernel(seq, u_ref, up_ref, un_ref, x_ref, mod_ref, wdw_ref, bdw_ref,
                  lng_ref, lnb_ref, w_ref, g_ref, o_ref, ext_ref, y_ref):
    tm, d = x_ref.shape
    j = pl.program_id(0)
    is_start = (j * tm) % seq == 0
    is_end = ((j + 1) * tm) % seq == 0
    ext_ref[0:CONV_HALO, :] = jnp.where(is_start, 0.0, up_ref[...])
    ext_ref[CONV_HALO:CONV_HALO + tm, :] = u_ref[...]
    ext_ref[CONV_HALO + tm:, :] = jnp.where(is_end, 0.0, un_ref[...])

    rows, cols = CONV_ROWS, CONV_COLS
    first = CONV_HALO - CONV_WIDTH // 2

    def row_block(r, carry):
        r0 = pl.multiple_of(r * rows, rows)
        for c in range(0, d, cols):
            n_win = (first + CONV_WIDTH - 1) // SUBLANES + 1
            wins = [ext_ref[pl.ds(r0 + SUBLANES * a, rows + SUBLANES), c:c + cols]
                    for a in range(n_win)]
            acc = None
            for b in range(SUBLANES):
                part = None
                for a in range((first + CONV_WIDTH - 1 - b) // SUBLANES + 1):
                    w = SUBLANES * a + b - first
                    if w < 0:
                        continue
                    term = wins[a] * wdw_ref[w:w + 1, c:c + cols]
                    part = term if part is None else part + term
                part = part[b:b + rows]
                acc = part if acc is None else acc + part
            y_ref[pl.ds(r0, rows), c:c + cols] = acc + bdw_ref[:, c:c + cols]
        return carry

    lax.fori_loop(0, tm // rows, row_block, 0)

    y = y_ref[...]
    mu = jnp.mean(y, axis=-1, keepdims=True)
    yc = y - mu
    z = yc * lax.rsqrt(jnp.mean(yc * yc, axis=-1, keepdims=True) + EPS)
    z = z * lng_ref[...] + lnb_ref[...]
    s = (z * _sigmoid(z)).astype(jnp.bfloat16)
    o_ref[...] = jnp.dot(s, w_ref[0], preferred_element_type=jnp.float32)
    _gated_residual(o_ref, x_ref, g_ref[...], mod_ref, 2)


def _conv2(u, x, mod, w_dw, b_dw, ln_g, ln_b, w_pw2, layer, g1, seq):
    t, d = x.shape
    tm = TOKEN_TILE
    per = tm // CONV_HALO
    last = t // CONV_HALO - 1
    return pl.pallas_call(
        functools.partial(_conv2_kernel, seq),
        grid=(t // tm,),
        in_specs=[
            pl.BlockSpec((tm, d), lambda j: (j, 0)),
            pl.BlockSpec((CONV_HALO, d), lambda j: (jnp.maximum(j * per - 1, 0), 0)),
            pl.BlockSpec((CONV_HALO, d), lambda j: (jnp.minimum((j + 1) * per, last), 0)),
            pl.BlockSpec((tm, d), lambda j: (j, 0)),
            pl.BlockSpec((1, 6, d), lambda j: (j * tm // seq, 0, 0)),
            _resident((CONV_WIDTH, d)),
            _resident((1, d)),
            _resident((1, d)),
            _resident((1, d)),
            _resident_layer((d, d), layer),
            _resident((1, d)),
        ],
        out_specs=pl.BlockSpec((tm, d), lambda j: (j, 0)),
        out_shape=jax.ShapeDtypeStruct((t, d), jnp.float32),
        scratch_shapes=[
            pltpu.VMEM((tm + 2 * CONV_HALO, d), jnp.float32),
            pltpu.VMEM((tm, d), jnp.float32),
        ],
        compiler_params=_params(1),
        name="conv_dw_pw2",
    )(u, u, u, x, mod, w_dw, b_dw, ln_g, ln_b, w_pw2, g1)


def _rope(x, cos, sin_signed):
    half = ROT_DIM // 2
    lane = lax.broadcasted_iota(jnp.int32, x.shape, 1)
    rot = jnp.where(lane < half,
                    pltpu.roll(x, HEAD_DIM - half, axis=1),
                    pltpu.roll(x, half, axis=1))
    return x * cos + rot * sin_signed


def _qkv_kernel(x_ref, mod_ref, g_ref, w_ref, cos_ref, sin_ref, q_ref, k_ref, vt_ref):
    nq = q_ref.shape[1]
    nk = k_ref.shape[1]
    h = _modulated_norm(x_ref[...], g_ref[...], mod_ref, 0, 1)
    cos = cos_ref[...]
    sin = sin_ref[...]
    chunk = 4 * HEAD_DIM
    for c in range(0, nq + nk, chunk):
        y = jnp.dot(h, w_ref[0, :, c:c + chunk], preferred_element_type=jnp.float32)
        for hd in range(0, chunk, HEAD_DIM):
            r = _rope(y[:, hd:hd + HEAD_DIM], cos, sin).astype(jnp.bfloat16)
            if c < nq:
                q_ref[:, c + hd:c + hd + HEAD_DIM] = r
            else:
                k_ref[:, c - nq + hd:c - nq + hd + HEAD_DIM] = r
    v = jnp.dot(h, w_ref[0, :, nq + nk:], preferred_element_type=jnp.float32)
    vt_ref[...] = v.T.astype(jnp.bfloat16)


def _qkv(x, mod, g0, w_qkv, layer, cos, sin, seq):
    t, d = x.shape
    tm = TOKEN_TILE
    nq = N_HEADS * HEAD_DIM
    nk = N_KV_HEADS * HEAD_DIM
    tiles_per_seq = seq // tm
    return pl.pallas_call(
        _qkv_kernel,
        grid=(t // tm,),
        in_specs=[
            pl.BlockSpec((tm, d), lambda j: (j, 0)),
            pl.BlockSpec((1, 6, d), lambda j: (j * tm // seq, 0, 0)),
            _resident((1, d)),
            _resident_layer((d, nq + 2 * nk), layer),
            pl.BlockSpec((tm, HEAD_DIM), lambda j: (j % tiles_per_seq, 0)),
            pl.BlockSpec((tm, HEAD_DIM), lambda j: (j % tiles_per_seq, 0)),
        ],
        out_specs=[
            pl.BlockSpec((tm, nq), lambda j: (j, 0)),
            pl.BlockSpec((tm, nk), lambda j: (j, 0)),
            pl.BlockSpec((nk, tm), lambda j: (0, j)),
        ],
        out_shape=[
            jax.ShapeDtypeStruct((t, nq), jnp.bfloat16),
            jax.ShapeDtypeStruct((t, nk), jnp.bfloat16),
            jax.ShapeDtypeStruct((nk, t), jnp.bfloat16),
        ],
        compiler_params=_params(1),
        name="attn_qkv_rope",
    )(x, mod, g0, w_qkv, cos, sin)


def _attn_kernel(seq, sink_ref, q_ref, k_ref, kp_ref, kn_ref, vt_ref, vtp_ref, vtn_ref,
                 x_ref, mod_ref, w_ref, g_ref, o_ref,
                 kext_ref, vtext_ref, oh_ref, bias_lo_ref, bias_hi_ref):
    tm = x_ref.shape[0]
    blk = WINDOW
    nblk = tm // blk
    cols = GROUP * blk
    j = pl.program_id(0)
    is_start = ((j * tm) % seq == 0).astype(jnp.int32)
    is_end = (((j + 1) * tm) % seq == 0).astype(jnp.int32)

    kext_ref[0:blk, :] = kp_ref[...]
    kext_ref[blk:blk + tm, :] = k_ref[...]
    kext_ref[blk + tm:, :] = kn_ref[...]
    vtext_ref[:, 0:blk] = vtp_ref[...]
    vtext_ref[:, blk:blk + tm] = vt_ref[...]
    vtext_ref[:, blk + tm:] = vtn_ref[...]

    kk = lax.broadcasted_iota(jnp.int32, (blk, cols), 0)
    qi = lax.broadcasted_iota(jnp.int32, (blk, cols), 1) & (blk - 1)
    bias_lo_ref[0] = jnp.where(kk >= qi, 0.0, NEG)
    bias_hi_ref[0] = jnp.where(kk <= qi, 0.0, NEG)
    bias_lo_ref[1] = jnp.full((blk, cols), NEG, jnp.float32)
    bias_hi_ref[1] = jnp.full((blk, cols), NEG, jnp.float32)
    scale = HEAD_DIM ** -0.5

    d = o_ref.shape[1]
    proj_cols = d // N_KV_HEADS
    for b in range(nblk + 1):
        lo_variant = is_start if b == 0 else 0
        hi_variant = is_end if b == nblk - 1 else 0
        for g in range(N_KV_HEADS):
            heads = range(g * GROUP, (g + 1) * GROUP)
            if b < nblk:
                qs = jnp.concatenate(
                    [q_ref[b * blk:(b + 1) * blk, h * HEAD_DIM:(h + 1) * HEAD_DIM] for h in heads],
                    axis=0)
                kb = kext_ref[b * blk:(b + 3) * blk, g * HEAD_DIM:(g + 1) * HEAD_DIM]
                vtb = vtext_ref[g * HEAD_DIM:(g + 1) * HEAD_DIM, b * blk:(b + 3) * blk]
                sc = lax.dot_general(kb, qs, (((1,), (1,)), ((), ())),
                                     preferred_element_type=jnp.float32) * scale
            if b > 0:
                o_ref[(b - 1) * blk:b * blk, g * proj_cols:(g + 1) * proj_cols] = jnp.dot(
                    oh_ref[(b - 1) * blk:b * blk, :], w_ref[0, :, g * proj_cols:(g + 1) * proj_cols],
                    preferred_element_type=jnp.float32)
            if b == nblk:
                continue
            s_lo = sc[0:blk] + bias_lo_ref[lo_variant]
            s_mid = sc[blk:2 * blk]
            s_hi = sc[2 * blk:] + bias_hi_ref[hi_variant]
            sink = jnp.concatenate(
                [jnp.full((1, blk), sink_ref[h], jnp.float32) for h in heads], axis=1)
            m = jnp.max(jnp.maximum(jnp.maximum(s_lo, s_mid), s_hi), axis=0, keepdims=True)
            m = jnp.maximum(m, sink)
            p_lo = jnp.exp(s_lo - m)
            p_mid = jnp.exp(s_mid - m)
            p_hi = jnp.exp(s_hi - m)
            denom = jnp.sum(p_lo + p_mid + p_hi, axis=0, keepdims=True) + jnp.exp(sink - m)
            p = jnp.concatenate([p_lo, p_mid, p_hi], axis=0).astype(jnp.bfloat16)
            ot = jnp.dot(vtb, p, preferred_element_type=jnp.float32) / denom
            for n, h in enumerate(heads):
                oh_ref[b * blk:(b + 1) * blk, h * HEAD_DIM:(h + 1) * HEAD_DIM] = (
                    ot[:, n * blk:(n + 1) * blk].T.astype(jnp.bfloat16))
        if b > 0:
            _gated_residual(o_ref, x_ref, g_ref[...], mod_ref, 2, (b - 1) * blk, blk)


def _attn(q, k, vt, x, mod, w_o, layer, g1, sink, seq):
    t, d = x.shape
    tm = TOKEN_TILE
    nq = q.shape[1]
    nk = k.shape[1]
    per = tm // WINDOW
    last = t // WINDOW - 1
    main = lambda j: (j, 0)
    prev = lambda j: (jnp.maximum(j * per - 1, 0), 0)
    nxt = lambda j: (jnp.minimum((j + 1) * per, last), 0)
    return pl.pallas_call(
        functools.partial(_attn_kernel, seq),
        grid=(t // tm,),
        in_specs=[
            pl.BlockSpec(memory_space=pltpu.SMEM),
            pl.BlockSpec((tm, nq), main),
            pl.BlockSpec((tm, nk), main),
            pl.BlockSpec((WINDOW, nk), prev),
            pl.BlockSpec((WINDOW, nk), nxt),
            pl.BlockSpec((nk, tm), lambda j: (0, j)),
            pl.BlockSpec((nk, WINDOW), lambda j: (0, jnp.maximum(j * per - 1, 0))),
            pl.BlockSpec((nk, WINDOW), lambda j: (0, jnp.minimum((j + 1) * per, last))),
            pl.BlockSpec((tm, d), main),
            pl.BlockSpec((1, 6, d), lambda j: (j * tm // seq, 0, 0)),
            _resident_layer((nq, d), layer),
            _resident((1, d)),
        ],
        out_specs=pl.BlockSpec((tm, d), main),
        out_shape=jax.ShapeDtypeStruct((t, d), jnp.float32),
        scratch_shapes=[
            pltpu.VMEM((tm + 2 * WINDOW, nk), jnp.bfloat16),
            pltpu.VMEM((nk, tm + 2 * WINDOW), jnp.bfloat16),
            pltpu.VMEM((tm, nq), jnp.bfloat16),
            pltpu.VMEM((2, WINDOW, GROUP * WINDOW), jnp.float32),
            pltpu.VMEM((2, WINDOW, GROUP * WINDOW), jnp.float32),
        ],
        compiler_params=_params(1),
        name="attn_band_out",
    )(sink, q, k, k, k, vt, vt, vt, x, mod, w_o, g1)


def _ffn_kernel(layer, x_ref, mod_ref, g2_ref, g3_ref, wg_hbm, wu_hbm, wd_hbm, o_ref,
                h_ref, wg_buf, wu_buf, wd_buf, sem):
    j = pl.program_id(0)
    tf = wg_buf.shape[2]
    nf = wg_hbm.shape[2] // tf
    total = pl.num_programs(0) * nf
    ahead = FFN_SLOTS

    def chunk_copies(f, slot):
        cols = pl.ds(pl.multiple_of(f * tf, tf), tf)
        return (
            pltpu.make_async_copy(wg_hbm.at[layer, :, cols], wg_buf.at[slot], sem.at[0, slot]),
            pltpu.make_async_copy(wu_hbm.at[layer, :, cols], wu_buf.at[slot], sem.at[1, slot]),
            pltpu.make_async_copy(wd_hbm.at[layer, cols, :], wd_buf.at[slot], sem.at[2, slot]),
        )

    @pl.when(j == 0)
    def _():
        for c in range(ahead):
            for cp in chunk_copies(c, c):
                cp.start()

    _modulated_norm_rows(h_ref, x_ref, g2_ref[...], mod_ref, 3, 4)
    o_ref[...] = jnp.zeros_like(o_ref)

    def chunk(f, carry):
        c = j * nf + f
        slot = lax.rem(c, ahead)
        for cp in chunk_copies(f, slot):
            cp.wait()
        h = h_ref[...]
        g = jnp.dot(h, wg_buf[slot], preferred_element_type=jnp.float32)
        u = jnp.dot(h, wu_buf[slot], preferred_element_type=jnp.float32)
        a = (g * _sigmoid(g) * u).astype(jnp.bfloat16)
        o_ref[...] += jnp.dot(a, wd_buf[slot], preferred_element_type=jnp.float32)

        @pl.when(c + ahead < total)
        def _():
            nxt = jnp.where(f + ahead >= nf, f + ahead - nf, f + ahead)
            for cp in chunk_copies(nxt, slot):
                cp.start()

        return carry

    lax.fori_loop(0, nf, chunk, 0)
    _gated_residual(o_ref, x_ref, g3_ref[...], mod_ref, 5)


def _ffn(x, mod, g2, g3, w_gate, w_up, w_down, layer, seq):
    t, d = x.shape
    tm = FFN_TOKEN_TILE
    tf = FF_TILE
    assert w_gate.shape[2] // tf >= FFN_SLOTS
    return pl.pallas_call(
        functools.partial(_ffn_kernel, layer),
        grid=(t // tm,),
        in_specs=[
            pl.BlockSpec((tm, d), lambda j: (j, 0)),
            pl.BlockSpec((1, 6, d), lambda j: (j * tm // seq, 0, 0)),
            _resident((1, d)),
            _resident((1, d)),
            pl.BlockSpec(memory_space=pl.ANY),
            pl.BlockSpec(memory_space=pl.ANY),
            pl.BlockSpec(memory_space=pl.ANY),
        ],
        out_specs=pl.BlockSpec((tm, d), lambda j: (j, 0)),
        out_shape=jax.ShapeDtypeStruct((t, d), jnp.float32),
        scratch_shapes=[
            pltpu.VMEM((tm, d), jnp.bfloat16),
            pltpu.VMEM((FFN_SLOTS, d, tf), jnp.bfloat16),
            pltpu.VMEM((FFN_SLOTS, d, tf), jnp.bfloat16),
            pltpu.VMEM((FFN_SLOTS, tf, d), jnp.bfloat16),
            pltpu.SemaphoreType.DMA((3, FFN_SLOTS)),
        ],
        compiler_params=_params(1),
        name="ffn_swiglu",
    )(x, mod, g2, g3, w_gate, w_up, w_down)


def _rope_tables(seq):
    inv_freq = ROPE_THETA ** (-jnp.arange(0, ROT_DIM, 2, dtype=jnp.float32) / ROT_DIM)
    ang = jnp.arange(seq, dtype=jnp.float32)[:, None] * inv_freq[None, :]
    pad = HEAD_DIM - ROT_DIM
    cos = jnp.concatenate([jnp.cos(ang), jnp.cos(ang), jnp.ones((seq, pad), jnp.float32)], -1)
    sin = jnp.concatenate([-jnp.sin(ang), jnp.sin(ang), jnp.zeros((seq, pad), jnp.float32)], -1)
    return cos, sin


def _trunk(x, mods, seq, weights):
    (norm_g, w_gate, w_up, w_down, conv_w_pw1, conv_w_dw, conv_b_dw, conv_ln_g, conv_ln_b,
     conv_w_pw2, attn_w_qkv, attn_w_o, attn_sink) = weights
    depth = norm_g.shape[0]
    d = x.shape[1]
    cos, sin = _rope_tables(seq)
    for i in range(depth):
        mod = mods[i]
        g = [norm_g[i, n].reshape(1, d) for n in range(4)]
        j = i // 2
        if i % 2 == 0:
            u = _conv1(x, mod, g[0], conv_w_pw1, j, seq)
            x = _conv2(u, x, mod, conv_w_dw[j], conv_b_dw[j].reshape(1, d),
                       conv_ln_g[j].reshape(1, d), conv_ln_b[j].reshape(1, d),
                       conv_w_pw2, j, g[1], seq)
        else:
            q, k, vt = _qkv(x, mod, g[0], attn_w_qkv, j, cos, sin, seq)
            x = _attn(q, k, vt, x, mod, attn_w_o, j, g[1], attn_sink[j], seq)
        x = _ffn(x, mod, g[2], g[3], w_gate, w_up, w_down, i, seq)
    return x


def kernel(x_prompt, x_sample, c_prompt, c_sample, w_ada, b_ada, norm_g, w_gate, w_up, w_down,
           conv_w_pw1, conv_w_dw, conv_b_dw, conv_ln_g, conv_ln_b, conv_w_pw2,
           attn_w_q, attn_w_k, attn_w_v, attn_w_o, attn_sink):
    bf16 = jnp.bfloat16
    depth, d, d_ff = w_gate.shape
    bp, sp, _ = x_prompt.shape
    bs, ss, _ = x_sample.shape
    assert sp % FFN_TOKEN_TILE == 0 and ss % FFN_TOKEN_TILE == 0 and d_ff % FF_TILE == 0
    assert FFN_TOKEN_TILE % TOKEN_TILE == 0 and TOKEN_TILE % WINDOW == 0
    assert TOKEN_TILE % CONV_ROWS == 0 and d % CONV_COLS == 0 and (6 * d) % ADA_TILE == 0

    n_rows = bp + bs
    rows = -(-n_rows // 8) * 8
    c_all = jnp.concatenate(
        [c_prompt, c_sample, jnp.zeros((rows - n_rows, d), jnp.float32)], axis=0)
    mod_all = _ada(c_all, w_ada, b_ada).reshape(depth, rows, 6, d)
    mods_p = [mod_all[i, :bp] for i in range(depth)]
    mods_s = [mod_all[i, bp:n_rows] for i in range(depth)]

    w_qkv = jnp.concatenate([attn_w_q, attn_w_k, attn_w_v], axis=-1).astype(bf16)
    weights = (norm_g, w_gate.astype(bf16), w_up.astype(bf16), w_down.astype(bf16),
               conv_w_pw1.astype(bf16), conv_w_dw, conv_b_dw,
               conv_ln_g, conv_ln_b, conv_w_pw2.astype(bf16), w_qkv,
               attn_w_o.astype(bf16), attn_sink)

    y_p = _trunk(x_prompt.reshape(bp * sp, d), mods_p, sp, weights)
    y_s = _trunk(x_sample.reshape(bs * ss, d), mods_s, ss, weights)
    return y_p.reshape(bp, sp, d), y_s.reshape(bs, ss, d)
```

```python
import functools

import jax
import jax.numpy as jnp
from jax import lax
from jax.experimental import pallas as pl
from jax.experimental.pallas import tpu as pltpu

N_HEADS = 16
N_KV_HEADS = 4
GROUP = N_HEADS // N_KV_HEADS
HEAD_DIM = 128
ROT_DIM = HEAD_DIM // 4
ROPE_THETA = 500000.0
WINDOW = 128
CONV_WIDTH = 31
CONV_HALO = 16
SUBLANES = 8
ROW_CHUNK = 16
CONV_ROWS = 128
CONV_COLS = 128
EPS = 1e-6
NEG = -1e30

TOKEN_TILE = 512
FFN_TOKEN_TILE = 512
FF_TILE = 512
FFN_SLOTS = 2
ADA_TILE = 1536
VMEM_LIMIT_BYTES = 56 * 1024 * 1024


def _params(n_axes):
    return pltpu.CompilerParams(
        dimension_semantics=("arbitrary",) * n_axes,
        vmem_limit_bytes=VMEM_LIMIT_BYTES)


def _resident(shape):
    return pl.BlockSpec(shape, lambda *_: (0,) * len(shape),
                        pipeline_mode=pl.Buffered(1))


def _resident_layer(shape, layer):
    return pl.BlockSpec((1,) + shape, lambda *_: (layer,) + (0,) * len(shape),
                        pipeline_mode=pl.Buffered(1))


def _sigmoid(x):
    return 1.0 / (1.0 + jnp.exp(-x))


def _rms(xf, g):
    return xf * lax.rsqrt(jnp.mean(xf * xf, axis=-1, keepdims=True) + EPS) * g


def _row_chunks(n_rows):
    return range(0, n_rows, ROW_CHUNK)


def _modulated_norm(x, g, mod_ref, shift_row, scale_row):
    shift = mod_ref[0, shift_row:shift_row + 1, :]
    scale = mod_ref[0, scale_row:scale_row + 1, :]
    return (_rms(x, g) * (1.0 + scale) + shift).astype(jnp.bfloat16)


def _modulated_norm_rows(h_ref, x_ref, g, mod_ref, shift_row, scale_row):
    shift = mod_ref[0, shift_row:shift_row + 1, :]
    gain = g * (1.0 + mod_ref[0, scale_row:scale_row + 1, :])
    for r in _row_chunks(x_ref.shape[0]):
        x = x_ref[r:r + ROW_CHUNK, :]
        h_ref[r:r + ROW_CHUNK, :] = (_rms(x, gain) + shift).astype(jnp.bfloat16)


def _gated_residual(o_ref, x_ref, g, mod_ref, gate_row, first_row=0, n_rows=None):
    gain = mod_ref[0, gate_row:gate_row + 1, :] * g
    n_rows = x_ref.shape[0] if n_rows is None else n_rows
    for r in range(first_row, first_row + n_rows, ROW_CHUNK):
        out = o_ref[r:r + ROW_CHUNK, :]
        o_ref[r:r + ROW_CHUNK, :] = x_ref[r:r + ROW_CHUNK, :] + _rms(out, gain)


def _ada_kernel(c_ref, w_ref, b_ref, o_ref):
    c = c_ref[...]
    ca = (c * _sigmoid(c)).astype(jnp.bfloat16)
    w = w_ref[0].astype(jnp.bfloat16)
    o_ref[0] = jnp.dot(ca, w, preferred_element_type=jnp.float32) + b_ref[0]


def _ada(c_all, w_ada, b_ada):
    depth, d, n = w_ada.shape
    rows = c_all.shape[0]
    return pl.pallas_call(
        _ada_kernel,
        grid=(depth, n // ADA_TILE),
        in_specs=[
            pl.BlockSpec((rows, d), lambda i, j: (0, 0)),
            pl.BlockSpec((1, d, ADA_TILE), lambda i, j: (i, 0, j)),
            pl.BlockSpec((1, 1, ADA_TILE), lambda i, j: (i, 0, j)),
        ],
        out_specs=pl.BlockSpec((1, rows, ADA_TILE), lambda i, j: (i, 0, j)),
        out_shape=jax.ShapeDtypeStruct((depth, rows, n), jnp.float32),
        compiler_params=_params(2),
        name="ada_mod",
    )(c_all, w_ada, b_ada.reshape(depth, 1, n))


def _conv1_kernel(x_ref, mod_ref, g_ref, w_ref, u_ref):
    d = x_ref.shape[1]
    h = _modulated_norm(x_ref[...], g_ref[...], mod_ref, 0, 1)
    chunk = 512
    for c in range(0, d, chunk):
        a = jnp.dot(h, w_ref[0, :, c:c + chunk], preferred_element_type=jnp.float32)
        g = jnp.dot(h, w_ref[0, :, d + c:d + c + chunk], preferred_element_type=jnp.float32)
        u_ref[:, c:c + chunk] = a * _sigmoid(g)


def _conv1(x, mod, g0, w_pw1, layer, seq):
    t, d = x.shape
    tm = TOKEN_TILE
    return pl.pallas_call(
        _conv1_kernel,
        grid=(t // tm,),
        in_specs=[
            pl.BlockSpec((tm, d), lambda j: (j, 0)),
            pl.BlockSpec((1, 6, d), lambda j: (j * tm // seq, 0, 0)),
            _resident((1, d)),
            _resident_layer((d, 2 * d), layer),
        ],
        out_specs=pl.BlockSpec((tm, d), lambda j: (j, 0)),
        out_shape=jax.ShapeDtypeStruct((t, d), jnp.float32),
        compiler_params=_params(1),
        name="conv_pw1_glu",
    )(x, mod, g0, w_pw1)


def _conv2_kernel(seq, u_ref, up_ref, un_ref, x_ref, mod_ref, wdw_ref, bdw_ref,
                  lng_ref, lnb_ref, w_ref, g_ref, o_ref, ext_ref, y_ref):
    tm, d = x_ref.shape
    j = pl.program_id(0)
    is_start = (j * tm) % seq == 0
    is_end = ((j + 1) * tm) % seq == 0
    ext_ref[0:CONV_HALO, :] = jnp.where(is_start, 0.0, up_ref[...])
    ext_ref[CONV_HALO:CONV_HALO + tm, :] = u_ref[...]
    ext_ref[CONV_HALO + tm:, :] = jnp.where(is_end, 0.0, un_ref[...])

    rows, cols = CONV_ROWS, CONV_COLS
    first = CONV_HALO - CONV_WIDTH // 2

    def row_block(r, carry):
        r0 = pl.multiple_of(r * rows, rows)
        for c in range(0, d, cols):
            acc = None
            for b in range(SUBLANES):
                part = None
                for a in range((first + CONV_WIDTH - 1 - b) // SUBLANES + 1):
                    w = SUBLANES * a + b - first
                    if w < 0:
                        continue
                    win = ext_ref[pl.ds(r0 + SUBLANES * a, rows + SUBLANES), c:c + cols]
                    term = win * wdw_ref[w:w + 1, c:c + cols]
                    part = term if part is None else part + term
                part = part[b:b + rows]
                acc = part if acc is None else acc + part
            y_ref[pl.ds(r0, rows), c:c + cols] = acc + bdw_ref[:, c:c + cols]
        return carry

    lax.fori_loop(0, tm // rows, row_block, 0)

    y = y_ref[...]
    mu = jnp.mean(y, axis=-1, keepdims=True)
    yc = y - mu
    z = yc * lax.rsqrt(jnp.mean(yc * yc, axis=-1, keepdims=True) + EPS)
    z = z * lng_ref[...] + lnb_ref[...]
    s = (z * _sigmoid(z)).astype(jnp.bfloat16)
    o_ref[...] = jnp.dot(s, w_ref[0], preferred_element_type=jnp.float32)
    _gated_residual(o_ref, x_ref, g_ref[...], mod_ref, 2)


def _conv2(u, x, mod, w_dw, b_dw, ln_g, ln_b, w_pw2, layer, g1, seq):
    t, d = x.shape
    tm = TOKEN_TILE
    per = tm // CONV_HALO
    last = t // CONV_HALO - 1
    return pl.pallas_call(
        functools.partial(_conv2_kernel, seq),
        grid=(t // tm,),
        in_specs=[
            pl.BlockSpec((tm, d), lambda j: (j, 0)),
            pl.BlockSpec((CONV_HALO, d), lambda j: (jnp.maximum(j * per - 1, 0), 0)),
            pl.BlockSpec((CONV_HALO, d), lambda j: (jnp.minimum((j + 1) * per, last), 0)),
            pl.BlockSpec((tm, d), lambda j: (j, 0)),
            pl.BlockSpec((1, 6, d), lambda j: (j * tm // seq, 0, 0)),
            _resident((CONV_WIDTH, d)),
            _resident((1, d)),
            _resident((1, d)),
            _resident((1, d)),
            _resident_layer((d, d), layer),
            _resident((1, d)),
        ],
        out_specs=pl.BlockSpec((tm, d), lambda j: (j, 0)),
        out_shape=jax.ShapeDtypeStruct((t, d), jnp.float32),
        scratch_shapes=[
            pltpu.VMEM((tm + 2 * CONV_HALO, d), jnp.float32),
            pltpu.VMEM((tm, d), jnp.float32),
        ],
        compiler_params=_params(1),
        name="conv_dw_pw2",
    )(u, u, u, x, mod, w_dw, b_dw, ln_g, ln_b, w_pw2, g1)


def _rope(x, cos, sin_signed):
    half = ROT_DIM // 2
    lane = lax.broadcasted_iota(jnp.int32, x.shape, 1)
    rot = jnp.where(lane < half,
                    pltpu.roll(x, HEAD_DIM - half, axis=1),
                    pltpu.roll(x, half, axis=1))
    return x * cos + rot * sin_signed


def _qkv_kernel(x_ref, mod_ref, g_ref, w_ref, cos_ref, sin_ref, q_ref, k_ref, vt_ref):
    nq = q_ref.shape[1]
    nk = k_ref.shape[1]
    h = _modulated_norm(x_ref[...], g_ref[...], mod_ref, 0, 1)
    cos = cos_ref[...]
    sin = sin_ref[...]
    chunk = 4 * HEAD_DIM
    for c in range(0, nq + nk, chunk):
        y = jnp.dot(h, w_ref[0, :, c:c + chunk], preferred_element_type=jnp.float32)
        for hd in range(0, chunk, HEAD_DIM):
            r = _rope(y[:, hd:hd + HEAD_DIM], cos, sin).astype(jnp.bfloat16)
            if c < nq:
                q_ref[:, c + hd:c + hd + HEAD_DIM] = r
            else:
                k_ref[:, c - nq + hd:c - nq + hd + HEAD_DIM] = r
    v = jnp.dot(h, w_ref[0, :, nq + nk:], preferred_element_type=jnp.float32)
    vt_ref[...] = v.T.astype(jnp.bfloat16)


def _qkv(x, mod, g0, w_qkv, layer, cos, sin, seq):
    t, d = x.shape
    tm = TOKEN_TILE
    nq = N_HEADS * HEAD_DIM
    nk = N_KV_HEADS * HEAD_DIM
    tiles_per_seq = seq // tm
    return pl.pallas_call(
        _qkv_kernel,
        grid=(t // tm,),
        in_specs=[
            pl.BlockSpec((tm, d), lambda j: (j, 0)),
            pl.BlockSpec((1, 6, d), lambda j: (j * tm // seq, 0, 0)),
            _resident((1, d)),
            _resident_layer((d, nq + 2 * nk), layer),
            pl.BlockSpec((tm, HEAD_DIM), lambda j: (j % tiles_per_seq, 0)),
            pl.BlockSpec((tm, HEAD_DIM), lambda j: (j % tiles_per_seq, 0)),
        ],
        out_specs=[
            pl.BlockSpec((tm, nq), lambda j: (j, 0)),
            pl.BlockSpec((tm, nk), lambda j: (j, 0)),
            pl.BlockSpec((nk, tm), lambda j: (0, j)),
        ],
        out_shape=[
            jax.ShapeDtypeStruct((t, nq), jnp.bfloat16),
            jax.ShapeDtypeStruct((t, nk), jnp.bfloat16),
            jax.ShapeDtypeStruct((nk, t), jnp.bfloat16),
        ],
        compiler_params=_params(1),
        name="attn_qkv_rope",
    )(x, mod, g0, w_qkv, cos, sin)


def _attn_kernel(seq, sink_ref, q_ref, k_ref, kp_ref, kn_ref, vt_ref, vtp_ref, vtn_ref,
                 x_ref, mod_ref, w_ref, g_ref, o_ref,
                 kext_ref, vtext_ref, oh_ref, bias_lo_ref, bias_hi_ref):
    tm = x_ref.shape[0]
    blk = WINDOW
    nblk = tm // blk
    cols = GROUP * blk
    j = pl.program_id(0)
    is_start = ((j * tm) % seq == 0).astype(jnp.int32)
    is_end = (((j + 1) * tm) % seq == 0).astype(jnp.int32)

    kext_ref[0:blk, :] = kp_ref[...]
    kext_ref[blk:blk + tm, :] = k_ref[...]
    kext_ref[blk + tm:, :] = kn_ref[...]
    vtext_ref[:, 0:blk] = vtp_ref[...]
    vtext_ref[:, blk:blk + tm] = vt_ref[...]
    vtext_ref[:, blk + tm:] = vtn_ref[...]

    kk = lax.broadcasted_iota(jnp.int32, (blk, cols), 0)
    qi = lax.broadcasted_iota(jnp.int32, (blk, cols), 1) & (blk - 1)
    bias_lo_ref[0] = jnp.where(kk >= qi, 0.0, NEG)
    bias_hi_ref[0] = jnp.where(kk <= qi, 0.0, NEG)
    bias_lo_ref[1] = jnp.full((blk, cols), NEG, jnp.float32)
    bias_hi_ref[1] = jnp.full((blk, cols), NEG, jnp.float32)
    scale = HEAD_DIM ** -0.5

    d = o_ref.shape[1]
    proj_cols = d // N_KV_HEADS
    for b in range(nblk + 1):
        lo_variant = is_start if b == 0 else 0
        hi_variant = is_end if b == nblk - 1 else 0
        for g in range(N_KV_HEADS):
            heads = range(g * GROUP, (g + 1) * GROUP)
            if b < nblk:
                qs = jnp.concatenate(
                    [q_ref[b * blk:(b + 1) * blk, h * HEAD_DIM:(h + 1) * HEAD_DIM] for h in heads],
                    axis=0)
                kb = kext_ref[b * blk:(b + 3) * blk, g * HEAD_DIM:(g + 1) * HEAD_DIM]
                vtb = vtext_ref[g * HEAD_DIM:(g + 1) * HEAD_DIM, b * blk:(b + 3) * blk]
                sc = lax.dot_general(kb, qs, (((1,), (1,)), ((), ())),
                                     preferred_element_type=jnp.float32) * scale
            if b > 0:
                o_ref[(b - 1) * blk:b * blk, g * proj_cols:(g + 1) * proj_cols] = jnp.dot(
                    oh_ref[(b - 1) * blk:b * blk, :], w_ref[0, :, g * proj_cols:(g + 1) * proj_cols],
                    preferred_element_type=jnp.float32)
            if b == nblk:
                continue
            s_lo = sc[0:blk] + bias_lo_ref[lo_variant]
            s_mid = sc[blk:2 * blk]
            s_hi = sc[2 * blk:] + bias_hi_ref[hi_variant]
            sink = jnp.concatenate(
                [jnp.full((1, blk), sink_ref[h], jnp.float32) for h in heads], axis=1)
            m = jnp.max(jnp.maximum(jnp.maximum(s_lo, s_mid), s_hi), axis=0, keepdims=True)
            m = jnp.maximum(m, sink)
            p_lo = jnp.exp(s_lo - m)
            p_mid = jnp.exp(s_mid - m)
            p_hi = jnp.exp(s_hi - m)
            denom = jnp.sum(p_lo + p_mid + p_hi, axis=0, keepdims=True) + jnp.exp(sink - m)
            p = jnp.concatenate([p_lo, p_mid, p_hi], axis=0).astype(jnp.bfloat16)
            ot = jnp.dot(vtb, p, preferred_element_type=jnp.float32) / denom
            for n, h in enumerate(heads):
                oh_ref[b * blk:(b + 1) * blk, h * HEAD_DIM:(h + 1) * HEAD_DIM] = (
                    ot[:, n * blk:(n + 1) * blk].T.astype(jnp.bfloat16))
        if b > 0:
            _gated_residual(o_ref, x_ref, g_ref[...], mod_ref, 2, (b - 1) * blk, blk)


def _attn(q, k, vt, x, mod, w_o, layer, g1, sink, seq):
    t, d = x.shape
    tm = TOKEN_TILE
    nq = q.shape[1]
    nk = k.shape[1]
    per = tm // WINDOW
    last = t // WINDOW - 1
    main = lambda j: (j, 0)
    prev = lambda j: (jnp.maximum(j * per - 1, 0), 0)
    nxt = lambda j: (jnp.minimum((j + 1) * per, last), 0)
    return pl.pallas_call(
        functools.partial(_attn_kernel, seq),
        grid=(t // tm,),
        in_specs=[
            pl.BlockSpec(memory_space=pltpu.SMEM),
            pl.BlockSpec((tm, nq), main),
            pl.BlockSpec((tm, nk), main),
            pl.BlockSpec((WINDOW, nk), prev),
            pl.BlockSpec((WINDOW, nk), nxt),
            pl.BlockSpec((nk, tm), lambda j: (0, j)),
            pl.BlockSpec((nk, WINDOW), lambda j: (0, jnp.maximum(j * per - 1, 0))),
            pl.BlockSpec((nk, WINDOW), lambda j: (0, jnp.minimum((j + 1) * per, last))),
            pl.BlockSpec((tm, d), main),
            pl.BlockSpec((1, 6, d), lambda j: (j * tm // seq, 0, 0)),
            _resident_layer((nq, d), layer),
            _resident((1, d)),
        ],
        out_specs=pl.BlockSpec((tm, d), main),
        out_shape=jax.ShapeDtypeStruct((t, d), jnp.float32),
        scratch_shapes=[
            pltpu.VMEM((tm + 2 * WINDOW, nk), jnp.bfloat16),
            pltpu.VMEM((nk, tm + 2 * WINDOW), jnp.bfloat16),
            pltpu.VMEM((tm, nq), jnp.bfloat16),
            pltpu.VMEM((2, WINDOW, GROUP * WINDOW), jnp.float32),
            pltpu.VMEM((2, WINDOW, GROUP * WINDOW), jnp.float32),
        ],
        compiler_params=_params(1),
        name="attn_band_out",
    )(sink, q, k, k, k, vt, vt, vt, x, mod, w_o, g1)


def _ffn_kernel(layer, x_ref, mod_ref, g2_ref, g3_ref, wg_hbm, wu_hbm, wd_hbm, o_ref,
                h_ref, wg_buf, wu_buf, wd_buf, sem):
    j = pl.program_id(0)
    tf = wg_buf.shape[2]
    nf = wg_hbm.shape[2] // tf
    total = pl.num_programs(0) * nf
    ahead = FFN_SLOTS

    def chunk_copies(f, slot):
        cols = pl.ds(pl.multiple_of(f * tf, tf), tf)
        return (
            pltpu.make_async_copy(wg_hbm.at[layer, :, cols], wg_buf.at[slot], sem.at[0, slot]),
            pltpu.make_async_copy(wu_hbm.at[layer, :, cols], wu_buf.at[slot], sem.at[1, slot]),
            pltpu.make_async_copy(wd_hbm.at[layer, cols, :], wd_buf.at[slot], sem.at[2, slot]),
        )

    @pl.when(j == 0)
    def _():
        for c in range(ahead):
            for cp in chunk_copies(c, c):
                cp.start()

    _modulated_norm_rows(h_ref, x_ref, g2_ref[...], mod_ref, 3, 4)
    o_ref[...] = jnp.zeros_like(o_ref)

    def chunk(f, carry):
        c = j * nf + f
        slot = lax.rem(c, ahead)
        for cp in chunk_copies(f, slot):
            cp.wait()
        h = h_ref[...]
        g = jnp.dot(h, wg_buf[slot], preferred_element_type=jnp.float32)
        u = jnp.dot(h, wu_buf[slot], preferred_element_type=jnp.float32)
        a = (g * _sigmoid(g) * u).astype(jnp.bfloat16)
        o_ref[...] += jnp.dot(a, wd_buf[slot], preferred_element_type=jnp.float32)

        @pl.when(c + ahead < total)
        def _():
            nxt = jnp.where(f + ahead >= nf, f + ahead - nf, f + ahead)
            for cp in chunk_copies(nxt, slot):
                cp.start()

        return carry

    lax.fori_loop(0, nf, chunk, 0)
    _gated_residual(o_ref, x_ref, g3_ref[...], mod_ref, 5)


def _ffn(x, mod, g2, g3, w_gate, w_up, w_down, layer, seq):
    t, d = x.shape
    tm = FFN_TOKEN_TILE
    tf = FF_TILE
    assert w_gate.shape[2] // tf >= FFN_SLOTS
    return pl.pallas_call(
        functools.partial(_ffn_kernel, layer),
        grid=(t // tm,),
        in_specs=[
            pl.BlockSpec((tm, d), lambda j: (j, 0)),
            pl.BlockSpec((1, 6, d), lambda j: (j * tm // seq, 0, 0)),
            _resident((1, d)),
            _resident((1, d)),
            pl.BlockSpec(memory_space=pl.ANY),
            pl.BlockSpec(memory_space=pl.ANY),
            pl.BlockSpec(memory_space=pl.ANY),
        ],
        out_specs=pl.BlockSpec((tm, d), lambda j: (j, 0)),
        out_shape=jax.ShapeDtypeStruct((t, d), jnp.float32),
        scratch_shapes=[
            pltpu.VMEM((tm, d), jnp.bfloat16),
            pltpu.VMEM((FFN_SLOTS, d, tf), jnp.bfloat16),
            pltpu.VMEM((FFN_SLOTS, d, tf), jnp.bfloat16),
            pltpu.VMEM((FFN_SLOTS, tf, d), jnp.bfloat16),
            pltpu.SemaphoreType.DMA((3, FFN_SLOTS)),
        ],
        compiler_params=_params(1),
        name="ffn_swiglu",
    )(x, mod, g2, g3, w_gate, w_up, w_down)


def _rope_tables(seq):
    inv_freq = ROPE_THETA ** (-jnp.arange(0, ROT_DIM, 2, dtype=jnp.float32) / ROT_DIM)
    ang = jnp.arange(seq, dtype=jnp.float32)[:, None] * inv_freq[None, :]
    pad = HEAD_DIM - ROT_DIM
    cos = jnp.concatenate([jnp.cos(ang), jnp.cos(ang), jnp.ones((seq, pad), jnp.float32)], -1)
    sin = jnp.concatenate([-jnp.sin(ang), jnp.sin(ang), jnp.zeros((seq, pad), jnp.float32)], -1)
    return cos, sin


def _trunk(x, mods, seq, weights):
    (norm_g, w_gate, w_up, w_down, conv_w_pw1, conv_w_dw, conv_b_dw, conv_ln_g, conv_ln_b,
     conv_w_pw2, attn_w_qkv, attn_w_o, attn_sink) = weights
    depth = norm_g.shape[0]
    d = x.shape[1]
    cos, sin = _rope_tables(seq)
    for i in range(depth):
        mod = mods[i]
        g = [norm_g[i, n].reshape(1, d) for n in range(4)]
        j = i // 2
        if i % 2 == 0:
            u = _conv1(x, mod, g[0], conv_w_pw1, j, seq)
            x = _conv2(u, x, mod, conv_w_dw[j], conv_b_dw[j].reshape(1, d),
                       conv_ln_g[j].reshape(1, d), conv_ln_b[j].reshape(1, d),
                       conv_w_pw2, j, g[1], seq)
        else:
            q, k, vt = _qkv(x, mod, g[0], attn_w_qkv, j, cos, sin, seq)
            x = _attn(q, k, vt, x, mod, attn_w_o, j, g[1], attn_sink[j], seq)
        x = _ffn(x, mod, g[2], g[3], w_gate, w_up, w_down, i, seq)
    return x


def kernel(x_prompt, x_sample, c_prompt, c_sample, w_ada, b_ada, norm_g, w_gate, w_up, w_down,
           conv_w_pw1, conv_w_dw, conv_b_dw, conv_ln_g, conv_ln_b, conv_w_pw2,
           attn_w_q, attn_w_k, attn_w_v, attn_w_o, attn_sink):
    bf16 = jnp.bfloat16
    depth, d, d_ff = w_gate.shape
    bp, sp, _ = x_prompt.shape
    bs, ss, _ = x_sample.shape
    assert sp % FFN_TOKEN_TILE == 0 and ss % FFN_TOKEN_TILE == 0 and d_ff % FF_TILE == 0
    assert FFN_TOKEN_TILE % TOKEN_TILE == 0 and TOKEN_TILE % WINDOW == 0
    assert TOKEN_TILE % CONV_ROWS == 0 and d % CONV_COLS == 0 and (6 * d) % ADA_TILE == 0

    n_rows = bp + bs
    rows = -(-n_rows // 8) * 8
    c_all = jnp.concatenate(
        [c_prompt, c_sample, jnp.zeros((rows - n_rows, d), jnp.float32)], axis=0)
    mod_all = _ada(c_all, w_ada, b_ada).reshape(depth, rows, 6, d)
    mods_p = [mod_all[i, :bp] for i in range(depth)]
    mods_s = [mod_all[i, bp:n_rows] for i in range(depth)]

    w_qkv = jnp.concatenate([attn_w_q, attn_w_k, attn_w_v], axis=-1).astype(bf16)
    weights = (norm_g, w_gate.astype(bf16), w_up.astype(bf16), w_down.astype(bf16),
               conv_w_pw1.astype(bf16), conv_w_dw, conv_b_dw,
               conv_ln_g, conv_ln_b, conv_w_pw2.astype(bf16), w_qkv,
               attn_w_o.astype(bf16), attn_sink)

    y_p = _trunk(x_prompt.reshape(bp * sp, d), mods_p, sp, weights)
    y_s = _trunk(x_sample.reshape(bs * ss, d), mods_s, ss, weights)
    return y_p.reshape(bp, sp, d), y_s.reshape(bs, ss, d)
```
